```python
import math
import jax, jax.numpy as jnp
from jax import lax
import numpy as np

D_MODEL = 1024
BATCH = 4
SEQ = 8192
DEPTH = 2
DEC_BATCH = 32
DEC_SEQ = 8
PAST_LEN = 16384
PAGE_SIZE = 128

N_A_LAYERS = DEPTH // 2
N_B_LAYERS = DEPTH - N_A_LAYERS
N_DENSE_LAYERS = (DEPTH + 1) // 2
N_MOE_LAYERS = DEPTH // 2
GLA_HEADS = 4
GLA_DK = D_MODEL // 2 // GLA_HEADS
GLA_DV = D_MODEL // GLA_HEADS
GLA_GATE_RANK = 16
GLA_GATE_NORM = 16.0
GLA_CHUNK = 64
GLA_IN = 2 * GLA_HEADS * GLA_DK + 2 * GLA_HEADS * GLA_DV + GLA_GATE_RANK
MOBA_HEADS = 8
MOBA_HEAD_DIM = D_MODEL // MOBA_HEADS
MOBA_BLOCK = 256
MOBA_TOPK = 3
MOBA_QCHUNK = 32
REL_BUCKETS = 32
REL_MAX_DIST = 4096
D_FF = 2816
N_EXPERTS = 8
MOE_TOP_K = 2
PLE_DIM = 256
RMS_EPS = 1e-6

kernel_name = 'yoco_gla_moba_decoder'


def rmsnorm(x, g):
    xf = x.astype(jnp.float32)
    y = xf * lax.rsqrt(jnp.mean(xf * xf, axis=-1, keepdims=True) + RMS_EPS)
    return (y * g.astype(jnp.float32)).astype(x.dtype)


def swiglu(x, w_gu, w_down):
    g, u = jnp.split(x @ w_gu, 2, axis=-1)
    return (jax.nn.silu(g) * u) @ w_down


def moe_swiglu(x, w_router, w_gu, w_down):
    logits = (x @ w_router).astype(jnp.float32)
    top_val, top_idx = lax.top_k(logits, MOE_TOP_K)
    top_w = jax.nn.softmax(top_val, axis=-1)
    combine = jnp.sum(jax.nn.one_hot(top_idx, N_EXPERTS, dtype=jnp.float32) * top_w[..., None], axis=-2)
    out = jnp.zeros_like(x)
    for e in range(N_EXPERTS):
        out = out + combine[..., e:e + 1].astype(x.dtype) * swiglu(x, w_gu[e], w_down[e])
    return out


def t5_bucket(dist):
    n = jnp.maximum(dist, 0)
    max_exact = REL_BUCKETS // 2
    large = max_exact + (jnp.log(jnp.maximum(n, 1).astype(jnp.float32) / max_exact)
                         / math.log(REL_MAX_DIST / max_exact) * (REL_BUCKETS - max_exact)).astype(jnp.int32)
    return jnp.where(n < max_exact, n, jnp.minimum(large, REL_BUCKETS - 1))


def gla_chunked(q, k, v, log_a, s0):
    f32 = jnp.float32
    B, L, H, DK = q.shape
    DV = v.shape[-1]
    C = math.gcd(L, GLA_CHUNK)
    N = L // C

    def chunks(t):
        return t.astype(f32).reshape(B, N, C, H, t.shape[-1]).transpose(1, 0, 3, 2, 4)

    qc, kc, vc, gc = chunks(q), chunks(k), chunks(v), chunks(log_a)
    b = jnp.cumsum(gc, axis=3)
    b_last = b[:, :, :, -1:, :]
    q_dec = qc * jnp.exp(b)
    k_inv = kc * jnp.exp(-b)
    k_end = kc * jnp.exp(b_last - b)
    causal = jnp.tril(jnp.ones((C, C), dtype=bool))
    attn = jnp.where(causal, jnp.einsum('nbhtk,nbhsk->nbhts', q_dec, k_inv), 0.0)
    o_intra = jnp.einsum('nbhts,nbhsv->nbhtv', attn, vc)

    def step(S, xs):
        q_n, k_n, v_n, dec_n = xs
        o_n = jnp.einsum('bhtk,bhkv->bhtv', q_n, S)
        S = dec_n[:, :, 0, :, None] * S + jnp.einsum('bhsk,bhsv->bhkv', k_n, v_n)
        return S, o_n

    s_final, o_inter = lax.scan(step, s0.astype(f32), (q_dec, k_end, vc, jnp.exp(b_last)))
    o = (o_intra + o_inter).transpose(1, 0, 3, 2, 4).reshape(B, L, H, DV)
    return o, s_final


def gla_mixer(xn, w_in, w_gate2, b_gate, g_onorm, w_out, s0):
    B, L, _ = xn.shape
    qk = GLA_HEADS * GLA_DK
    vd = GLA_HEADS * GLA_DV
    proj = xn @ w_in
    q, k, v, r, g_lr = jnp.split(proj, [qk, 2 * qk, 2 * qk + vd, 2 * qk + 2 * vd], axis=-1)
    log_a = jax.nn.log_sigmoid((g_lr @ w_gate2 + b_gate).astype(jnp.float32)) / GLA_GATE_NORM

    def heads(t, d):
        return t.reshape(B, L, GLA_HEADS, d)

    o, s = gla_chunked(heads(q, GLA_DK) * (GLA_DK ** -0.5), heads(k, GLA_DK), heads(v, GLA_DV),
                       heads(log_a, GLA_DK), s0)
    o = rmsnorm(o, g_onorm) * jax.nn.silu(heads(r, GLA_DV).astype(jnp.float32))
    return o.reshape(B, L, vd).astype(xn.dtype) @ w_out, s


def _pad_gate(gate):
    n = gate.shape[-1]
    if n < MOBA_TOPK:
        gate = jnp.pad(gate, [(0, 0)] * (gate.ndim - 1) + [(0, MOBA_TOPK - n)], constant_values=-jnp.inf)
    return gate


def _moba_attend(qf, qpos, ksel, vsel, kpos_sel, valid, kown, vown, kpos_own, bias_hb):
    B, H, Q, HD = qf.shape
    scale = HD ** -0.5
    hi = jnp.arange(H).reshape(1, H, 1, 1, 1)
    d_sel = qpos[:, None, None] - kpos_sel
    l_sel = jnp.einsum('bhqd,bhqjkd->bhqjk', qf, ksel) * scale + bias_hb[hi, t5_bucket(d_sel)]
    l_sel = jnp.where(valid[..., None], l_sel, -jnp.inf).reshape(B, H, Q, -1)
    d_own = qpos[:, None] - kpos_own[None, :]
    l_own = jnp.einsum('bhqd,bhkd->bhqk', qf, kown) * scale + bias_hb[:, t5_bucket(d_own)]
    l_own = jnp.where(d_own >= 0, l_own, -jnp.inf)
    n_sel = l_sel.shape[-1]
    probs = jax.nn.softmax(jnp.concatenate([l_sel, l_own], axis=-1), axis=-1)
    p_sel = probs[..., :n_sel].reshape(ksel.shape[:-1])
    return (jnp.einsum('bhqjk,bhqjkd->bhqd', p_sel, vsel)
            + jnp.einsum('bhqk,bhkd->bhqd', probs[..., n_sel:], vown))


def moba_prompt(q, k, v, rel_bias):
    f32 = jnp.float32
    B, S, H, HD = q.shape
    nblk = -(-S // MOBA_BLOCK)
    pad = nblk * MOBA_BLOCK - S

    def blocks(t):
        t = jnp.pad(t.astype(f32), ((0, 0), (0, pad), (0, 0), (0, 0)))
        return t.reshape(B, nblk, MOBA_BLOCK, H, HD).transpose(0, 3, 1, 2, 4)

    kb, vb = blocks(k), blocks(v)
    k_mean = jnp.mean(kb, axis=3)
    bias_hb = rel_bias.T.astype(f32)
    bi = jnp.arange(B).reshape(B, 1, 1, 1)
    hi = jnp.arange(H).reshape(1, H, 1, 1)
    offs = jnp.arange(MOBA_BLOCK)

    def one_chunk(c):
        q0 = c * MOBA_QCHUNK
        qf = lax.dynamic_slice_in_dim(q, q0, MOBA_QCHUNK, axis=1).transpose(0, 2, 1, 3).astype(f32)
        qpos = q0 + jnp.arange(MOBA_QCHUNK)
        own = q0 // MOBA_BLOCK
        gate = jnp.einsum('bhqd,bhnd->bhqn', qf, k_mean)
        gate = jnp.where(jnp.arange(nblk) < own, gate, -jnp.inf)
        _, idx = lax.top_k(_pad_gate(gate), MOBA_TOPK)
        valid = idx < own
        idx = jnp.minimum(idx, nblk - 1)
        ksel, vsel = kb[bi, hi, idx], vb[bi, hi, idx]
        kpos = idx[..., None] * MOBA_BLOCK + offs
        kown = lax.dynamic_index_in_dim(kb, own, axis=2, keepdims=False)
        vown = lax.dynamic_index_in_dim(vb, own, axis=2, keepdims=False)
        return _moba_attend(qf, qpos, ksel, vsel, kpos, valid, kown, vown, own * MOBA_BLOCK + offs, bias_hb)

    out = lax.map(one_chunk, jnp.arange(S // MOBA_QCHUNK))
    return out.transpose(1, 0, 3, 2, 4).reshape(B, S, H, HD).astype(q.dtype)


def moba_sample(q, k_new, v_new, cache_k, cache_v, page_table, rel_bias):
    f32 = jnp.float32
    DB, L, H, HD = q.shape
    n_pages = page_table.shape[1]
    past = n_pages * PAGE_SIZE
    ppb = MOBA_BLOCK // PAGE_SIZE
    nb = past // MOBA_BLOCK
    own_start = nb * MOBA_BLOCK
    n_own_pages = (past - own_start) // PAGE_SIZE
    bias_hb = rel_bias.T.astype(f32)
    qf = q.transpose(0, 2, 1, 3).astype(f32)
    qpos = past + jnp.arange(L)
    page_mean = jnp.mean(cache_k.astype(f32), axis=1)
    k_mean = page_mean[page_table[:, :nb * ppb]].reshape(DB, nb, ppb, H, HD).mean(axis=2)
    gate = jnp.einsum('bhqd,bnhd->bhqn', qf, k_mean)
    _, idx = lax.top_k(_pad_gate(gate), MOBA_TOPK)
    valid = idx < nb
    idx = jnp.clip(idx, 0, max(nb - 1, 0))
    lpage = jnp.minimum(idx[..., None] * ppb + jnp.arange(ppb), n_pages - 1)
    bi = jnp.arange(DB).reshape(DB, 1, 1, 1, 1)
    hi = jnp.arange(H).reshape(1, H, 1, 1, 1)
    phys = page_table[bi, lpage]
    ksel = cache_k[phys, :, hi].reshape(DB, H, L, MOBA_TOPK, MOBA_BLOCK, HD).astype(f32)
    vsel = cache_v[phys, :, hi].reshape(DB, H, L, MOBA_TOPK, MOBA_BLOCK, HD).astype(f32)
    kpos = (lpage[..., None] * PAGE_SIZE + jnp.arange(PAGE_SIZE)).reshape(DB, H, L, MOBA_TOPK, MOBA_BLOCK)
    k_own, v_own = k_new.astype(f32), v_new.astype(f32)
    if n_own_pages > 0:
        own_phys = page_table[:, n_pages - n_own_pages:]
        k_own = jnp.concatenate(
            [cache_k[own_phys].reshape(DB, n_own_pages * PAGE_SIZE, H, HD).astype(f32), k_own], axis=1)
        v_own = jnp.concatenate(
            [cache_v[own_phys].reshape(DB, n_own_pages * PAGE_SIZE, H, HD).astype(f32), v_own], axis=1)
    kpos_own = own_start + jnp.arange(k_own.shape[1])
    out = _moba_attend(qf, qpos, ksel, vsel, kpos, valid, k_own.transpose(0, 2, 1, 3),
                       v_own.transpose(0, 2, 1, 3), kpos_own, bias_hb)
    return out.transpose(0, 2, 1, 3).astype(q.dtype)


def setup_inputs(seed: int = 0) -> dict:
    key = jax.random.key(seed)
    ks = iter(jax.random.split(key, 40))
    f32 = jnp.float32

    def nrm(shape, scale):
        return jax.random.normal(next(ks), shape, f32) * scale

    def gain(shape):
        return 1.0 + nrm(shape, 0.01)

    n_pages = PAST_LEN // PAGE_SIZE
    n_used = DEC_BATCH * n_pages
    n_phys = n_used + max(1, n_used // 4)
    page_table = jax.random.permutation(next(ks), n_phys)[:n_used].reshape(DEC_BATCH, n_pages).astype(jnp.int32)
    hd = MOBA_HEADS * MOBA_HEAD_DIM
    return {
        'x_prompt': nrm((BATCH, SEQ, D_MODEL), 1.0),
        'x_sample': nrm((DEC_BATCH, DEC_SEQ, D_MODEL), 1.0),
        'state_gla': nrm((N_A_LAYERS, DEC_BATCH, GLA_HEADS, GLA_DK, GLA_DV), 1.0),
        'cache_k': nrm((n_phys, PAGE_SIZE, MOBA_HEADS, MOBA_HEAD_DIM), 1.0),
        'cache_v': nrm((n_phys, PAGE_SIZE, MOBA_HEADS, MOBA_HEAD_DIM), 1.0),
        'page_table': page_table,
        'p_prompt': nrm((DEPTH, BATCH, SEQ, PLE_DIM), 1.0),
        'p_sample': nrm((DEPTH, DEC_BATCH, DEC_SEQ, PLE_DIM), 1.0),
        'g_mix': gain((DEPTH, D_MODEL)),
        'g_ffn': gain((DEPTH, D_MODEL)),
        'g_ple': gain((DEPTH, D_MODEL)),
        'w_a_in': nrm((N_A_LAYERS, D_MODEL, GLA_IN), D_MODEL ** -0.5),
        'w_a_gate2': nrm((N_A_LAYERS, GLA_GATE_RANK, GLA_HEADS * GLA_DK), GLA_GATE_RANK ** -0.5),
        'b_a_gate': nrm((N_A_LAYERS, GLA_HEADS * GLA_DK), 0.1),
        'g_a_onorm': gain((N_A_LAYERS, GLA_DV)),
        'w_a_out': nrm((N_A_LAYERS, GLA_HEADS * GLA_DV, D_MODEL), (GLA_HEADS * GLA_DV) ** -0.5),
        'g_kv': gain((D_MODEL,)),
        'w_kv': nrm((D_MODEL, 2 * hd), D_MODEL ** -0.5),
        'w_b_q': nrm((N_B_LAYERS, D_MODEL, hd), D_MODEL ** -0.5),
        'w_b_out': nrm((N_B_LAYERS, hd, D_MODEL), hd ** -0.5),
        'rel_bias': nrm((REL_BUCKETS, MOBA_HEADS), 0.5),
        'w_ffn_gu': nrm((N_DENSE_LAYERS, D_MODEL, 2 * D_FF), D_MODEL ** -0.5),
        'w_ffn_down': nrm((N_DENSE_LAYERS, D_FF, D_MODEL), D_FF ** -0.5),
        'w_router': nrm((N_MOE_LAYERS, D_MODEL, N_EXPERTS), D_MODEL ** -0.5),
        'w_moe_gu': nrm((N_MOE_LAYERS, N_EXPERTS, D_MODEL, 2 * D_FF), D_MODEL ** -0.5),
        'w_moe_down': nrm((N_MOE_LAYERS, N_EXPERTS, D_FF, D_MODEL), D_FF ** -0.5),
        'w_ple_proj': nrm((DEPTH, PLE_DIM, D_MODEL), PLE_DIM ** -0.5),
        'w_ple_gate': nrm((DEPTH, D_MODEL, D_MODEL), D_MODEL ** -0.5),
        'g_final': gain((D_MODEL,)),
    }


def reference(x_prompt, x_sample, state_gla, cache_k, cache_v, page_table, p_prompt, p_sample,
              g_mix, g_ffn, g_ple, w_a_in, w_a_gate2, b_a_gate, g_a_onorm, w_a_out, g_kv, w_kv,
              w_b_q, w_b_out, rel_bias, w_ffn_gu, w_ffn_down, w_router, w_moe_gu, w_moe_down,
              w_ple_proj, w_ple_gate, g_final):

    def trunk(x, p, gla_states, attend):
        B, L, _ = x.shape
        h = x
        new_states = []
        k_sh = v_sh = None
        for i in range(DEPTH):
            xn = rmsnorm(h, g_mix[i])
            if i < N_A_LAYERS:
                y, s = gla_mixer(xn, w_a_in[i], w_a_gate2[i], b_a_gate[i], g_a_onorm[i], w_a_out[i], gla_states[i])
                new_states.append(s)
            else:
                j = i - N_A_LAYERS
                q = (xn @ w_b_q[j]).reshape(B, L, MOBA_HEADS, MOBA_HEAD_DIM)
                y = attend(q, k_sh, v_sh).reshape(B, L, MOBA_HEADS * MOBA_HEAD_DIM) @ w_b_out[j]
            h = h + y
            xn = rmsnorm(h, g_ffn[i])
            if i % 2 == 0:
                h = h + swiglu(xn, w_ffn_gu[i // 2], w_ffn_down[i // 2])
            else:
                h = h + moe_swiglu(xn, w_router[i // 2], w_moe_gu[i // 2], w_moe_down[i // 2])
            gate = jax.nn.sigmoid(rmsnorm(h, g_ple[i]) @ w_ple_gate[i])
            h = h + gate * (p[i] @ w_ple_proj[i])
            if i == N_A_LAYERS - 1:
                kv = (rmsnorm(h, g_kv) @ w_kv).reshape(B, L, 2, MOBA_HEADS, MOBA_HEAD_DIM)
                k_sh, v_sh = kv[:, :, 0], kv[:, :, 1]
        return rmsnorm(h, g_final), jnp.stack(new_states), k_sh, v_sh

    def attend_prompt(q, k, v):
        return moba_prompt(q, k, v, rel_bias)

    def attend_sample(q, k, v):
        return moba_sample(q, k, v, cache_k, cache_v, page_table, rel_bias)

    zeros = jnp.zeros((N_A_LAYERS, x_prompt.shape[0], GLA_HEADS, GLA_DK, GLA_DV), jnp.float32)
    y_prompt, st_prompt, k_prompt, v_prompt = trunk(x_prompt, p_prompt, zeros, attend_prompt)
    y_sample, st_sample, k_sample, v_sample = trunk(x_sample, p_sample, state_gla, attend_sample)
    return (y_prompt, y_sample, st_prompt, st_sample, k_prompt, v_prompt, k_sample, v_sample)
```

```python
import functools
import math

import jax
import jax.numpy as jnp
from jax import lax
from jax.experimental import pallas as pl
from jax.experimental.pallas import tpu as pltpu

F32 = jnp.float32
BF16 = jnp.bfloat16
HI = lax.Precision.HIGHEST

GLA_HEADS = 4
GLA_GATE_NORM = 16.0
GLA_CHUNK = 64
MOBA_HEADS = 8
MOBA_BLOCK = 256
MOBA_TOPK = 3
REL_BUCKETS = 32
REL_MAX_DIST = 4096
MOE_TOP_K = 2
RMS_EPS = 1e-6

LANES = 128
VMEM_LIMIT = 56 * 1024 * 1024
NEG = -1e30

NT = (((1,), (1,)), ((), ()))
TN = (((0,), (0,)), ((), ()))


def _cparams(*sem):
    return pltpu.CompilerParams(dimension_semantics=sem, vmem_limit_bytes=VMEM_LIMIT)


def _token_tile(t, cap):
    tm = cap
    while t % tm:
        tm //= 2
    return tm


def _rms(x, g):
    return x * lax.rsqrt(jnp.mean(x * x, axis=-1, keepdims=True) + RMS_EPS) * g


def _sigmoid(x):
    return 1.0 / (1.0 + jnp.exp(-x))


def _silu(x):
    return x * _sigmoid(x)


def _row(v):
    return v.reshape(1, -1).astype(F32)


def _gla_in_kernel(h_ref, g_ref, w_ref, wlr_ref, wg2_ref, bg_ref,
                   q_ref, k_ref, v_ref, r_ref, la_ref, *, qk, vd, qscale):
    xn = _rms(h_ref[...], g_ref[...]).astype(BF16)
    proj = jnp.dot(xn, w_ref[...], preferred_element_type=F32)
    q_ref[...] = proj[:, :qk] * qscale
    k_ref[...] = proj[:, qk:2 * qk]
    v_ref[...] = proj[:, 2 * qk:2 * qk + vd]
    r_ref[...] = proj[:, 2 * qk + vd:]
    glr = jnp.dot(xn, wlr_ref[...], preferred_element_type=F32)
    z = jnp.dot(glr, wg2_ref[...], precision=HI, preferred_element_type=F32) + bg_ref[...]
    log_sig = jnp.minimum(z, 0.0) - jnp.log1p(jnp.exp(-jnp.abs(z)))
    la_ref[...] = log_sig / GLA_GATE_NORM


def _gla_in(h, g, w_in, w_gate2, b_gate, tm):
    t, d = h.shape
    qk = w_gate2.shape[1]
    rank = w_gate2.shape[0]
    vd = (w_in.shape[1] - 2 * qk - rank) // 2
    dk = qk // GLA_HEADS
    n_main = 2 * qk + 2 * vd
    w_main = w_in[:, :n_main].astype(BF16)
    w_lr = w_in[:, n_main:].astype(BF16)
    tok = lambda n: pl.BlockSpec((tm, n), lambda i: (i, 0))
    full = lambda a: pl.BlockSpec(a.shape, lambda i: (0,) * a.ndim)
    args = (h, _row(g), w_main, w_lr, w_gate2.astype(F32), _row(b_gate))
    return pl.pallas_call(
        functools.partial(_gla_in_kernel, qk=qk, vd=vd, qscale=dk ** -0.5),
        grid=(t // tm,),
        in_specs=[tok(d)] + [full(a) for a in args[1:]],
        out_specs=[tok(qk), tok(qk), tok(vd), tok(vd), tok(qk)],
        out_shape=[jax.ShapeDtypeStruct((t, n), F32) for n in (qk, qk, vd, vd, qk)],
        compiler_params=_cparams("parallel"),
        name="gla_in",
    )(*args)


def _gla_kernel(q_ref, k_ref, v_ref, la_ref, s0_ref, o_ref, sout_ref, st_ref, *, chunk, n_chunks, mm_dtype):
    t = pl.program_id(2)

    @pl.when(t == 0)
    def _():
        st_ref[...] = s0_ref[0, 0].T

    rows = lax.broadcasted_iota(jnp.int32, (chunk, chunk), 0)
    cols = lax.broadcasted_iota(jnp.int32, (chunk, chunk), 1)
    tri = rows >= cols
    tri_f = tri.astype(F32)
    for c in range(n_chunks):
        sl = slice(c * chunk, (c + 1) * chunk)
        b = jnp.dot(tri_f, la_ref[0, sl, :], precision=HI, preferred_element_type=F32)
        b_last = b[chunk - 1:chunk, :]
        q = q_ref[0, sl, :]
        k = k_ref[0, sl, :]
        v = v_ref[0, sl, :].astype(mm_dtype)
        q_dec = (q * jnp.exp(b)).astype(mm_dtype)
        k_inv = (k * jnp.exp(-b)).astype(mm_dtype)
        k_end = (k * jnp.exp(b_last - b)).astype(mm_dtype)
        attn = lax.dot_general(q_dec, k_inv, NT, preferred_element_type=F32)
        attn = jnp.where(tri, attn, 0.0).astype(mm_dtype)
        st = st_ref[...]
        o = jnp.dot(attn, v, preferred_element_type=F32)
        o = o + lax.dot_general(q_dec, st.astype(mm_dtype), NT, preferred_element_type=F32)
        o_ref[0, sl, :] = o
        st_ref[...] = st * jnp.exp(b_last) + lax.dot_general(v, k_end, TN, preferred_element_type=F32)

    @pl.when(t == pl.num_programs(2) - 1)
    def _():
        sout_ref[0, 0] = st_ref[...].T


def _gla_chunked(q, k, v, la, s0):
    b, l, qk = q.shape
    dk = qk // GLA_HEADS
    dv = v.shape[2] // GLA_HEADS
    chunk = math.gcd(l, GLA_CHUNK)
    tl = _token_tile(l, 512)
    n_chunks = tl // chunk
    mm_dtype = BF16 if chunk % 16 == 0 else F32
    kspec = pl.BlockSpec((1, tl, dk), lambda bi, hi, ti: (bi, ti, hi))
    vspec = pl.BlockSpec((1, tl, dv), lambda bi, hi, ti: (bi, ti, hi))
    sspec = pl.BlockSpec((1, 1, dk, dv), lambda bi, hi, ti: (bi, hi, 0, 0))
    return pl.pallas_call(
        functools.partial(_gla_kernel, chunk=chunk, n_chunks=n_chunks, mm_dtype=mm_dtype),
        grid=(b, GLA_HEADS, l // tl),
        in_specs=[kspec, kspec, vspec, kspec, sspec],
        out_specs=[vspec, sspec],
        out_shape=[jax.ShapeDtypeStruct(v.shape, F32), jax.ShapeDtypeStruct(s0.shape, F32)],
        scratch_shapes=[pltpu.VMEM((dv, dk), F32)],
        compiler_params=_cparams("parallel", "parallel", "arbitrary"),
        name="gla_chunk",
    )(q, k, v, la, s0.astype(F32))


def _gla_out_kernel(o_ref, r_ref, h_ref, g_ref, w_ref, out_ref, *, dv):
    g = g_ref[...]
    parts = []
    for hd in range(GLA_HEADS):
        sl = slice(hd * dv, (hd + 1) * dv)
        parts.append(_rms(o_ref[:, sl], g) * _silu(r_ref[:, sl]))
    y = jnp.concatenate(parts, axis=1).astype(BF16)
    out_ref[...] = h_ref[...] + jnp.dot(y, w_ref[...], preferred_element_type=F32)


def _gla_out(o, r, h, g_onorm, w_out, tm):
    t, d = h.shape
    vd = o.shape[1]
    tok = lambda n: pl.BlockSpec((tm, n), lambda i: (i, 0))
    full = lambda shape: pl.BlockSpec(shape, lambda i: (0,) * len(shape))
    return pl.pallas_call(
        functools.partial(_gla_out_kernel, dv=vd // GLA_HEADS),
        grid=(t // tm,),
        in_specs=[tok(vd), tok(vd), tok(d), full((1, vd // GLA_HEADS)), full(w_out.shape)],
        out_specs=tok(d),
        out_shape=jax.ShapeDtypeStruct((t, d), F32),
        compiler_params=_cparams("parallel"),
        name="gla_out",
    )(o, r, h, _row(g_onorm), w_out.astype(BF16))


def _ffn_kernel(h_ref, g_ref, wg_ref, wu_ref, wd_ref, out_ref, xn_ref, acc_ref):
    j = pl.program_id(1)

    @pl.when(j == 0)
    def _():
        xn_ref[...] = _rms(h_ref[...], g_ref[...]).astype(BF16)
        acc_ref[...] = jnp.zeros_like(acc_ref)

    xn = xn_ref[...]
    gate = jnp.dot(xn, wg_ref[...], preferred_element_type=F32)
    up = jnp.dot(xn, wu_ref[...], preferred_element_type=F32)
    act = (_silu(gate) * up).astype(BF16)
    acc_ref[...] += jnp.dot(act, wd_ref[...], preferred_element_type=F32)

    @pl.when(j == pl.num_programs(1) - 1)
    def _():
        out_ref[...] = h_ref[...] + acc_ref[...]


def _ff_tile(f):
    for nf in (1, 2, 4, 11, 22):
        if f % nf == 0 and (f // nf) % LANES == 0 and f // nf <= 1536:
            return f // nf
    return f


def _ffn(h, g, w_gu, w_down, tm):
    t, d = h.shape
    f = w_down.shape[0]
    tf = _ff_tile(f)
    nf = f // tf
    w_gu = w_gu.astype(BF16)
    return pl.pallas_call(
        _ffn_kernel,
        grid=(t // tm, nf),
        in_specs=[pl.BlockSpec((tm, d), lambda i, j: (i, 0)),
                  pl.BlockSpec((1, d), lambda i, j: (0, 0)),
                  pl.BlockSpec((d, tf), lambda i, j: (0, j)),
                  pl.BlockSpec((d, tf), lambda i, j: (0, nf + j)),
                  pl.BlockSpec((tf, d), lambda i, j: (j, 0))],
        out_specs=pl.BlockSpec((tm, d), lambda i, j: (i, 0)),
        out_shape=jax.ShapeDtypeStruct((t, d), F32),
        scratch_shapes=[pltpu.VMEM((tm, d), BF16), pltpu.VMEM((tm, d), F32)],
        compiler_params=_cparams("parallel", "arbitrary"),
        name="ffn",
    )(h, _row(g), w_gu, w_gu, w_down.astype(BF16))


def _ple_kernel(*refs, has_w):
    h_ref, p_ref, g_ref, wgate_ref, wproj_ref = refs[:5]
    n_heads = len(has_w)
    head_refs = refs[5:5 + sum(2 if w else 1 for w in has_w)]
    out_refs = refs[5 + len(head_refs):]
    h = h_ref[...]
    gate = _sigmoid(jnp.dot(_rms(h, g_ref[...]).astype(BF16), wgate_ref[...], preferred_element_type=F32))
    emb = jnp.dot(p_ref[...].astype(BF16), wproj_ref[...], preferred_element_type=F32)
    h = h + gate * emb
    out_refs[0][...] = h
    pos = 0
    for i in range(n_heads):
        y = _rms(h, head_refs[pos][...])
        if has_w[i]:
            y = jnp.dot(y.astype(BF16), head_refs[pos + 1][...], preferred_element_type=F32)
            pos += 2
        else:
            pos += 1
        out_refs[1 + i][...] = y


def _ple(h, p, g, w_gate, w_proj, heads, tm):
    t, d = h.shape
    tok = lambda n: pl.BlockSpec((tm, n), lambda i: (i, 0))
    full = lambda a: pl.BlockSpec(a.shape, lambda i: (0,) * a.ndim)
    args = [h, p, _row(g), w_gate.astype(BF16), w_proj.astype(BF16)]
    has_w = []
    out_dims = [d]
    for gain, w in heads:
        args.append(_row(gain))
        has_w.append(w is not None)
        if w is not None:
            args.append(w.astype(BF16))
            out_dims.append(w.shape[1])
        else:
            out_dims.append(d)
    return pl.pallas_call(
        functools.partial(_ple_kernel, has_w=tuple(has_w)),
        grid=(t // tm,),
        in_specs=[tok(d), tok(p.shape[1])] + [full(a) for a in args[2:]],
        out_specs=[tok(n) for n in out_dims],
        out_shape=[jax.ShapeDtypeStruct((t, n), F32) for n in out_dims],
        compiler_params=_cparams("parallel"),
        name="ple",
    )(*args)


def _router_kernel(a_ref, h_ref, wo_ref, g_ref, wr_ref, out_ref, comb_ref, *, n_experts):
    h = h_ref[...] + jnp.dot(a_ref[...].astype(BF16), wo_ref[...], preferred_element_type=F32)
    out_ref[...] = h
    logits = jnp.dot(_rms(h, g_ref[...]), wr_ref[...], precision=HI, preferred_element_type=F32)
    lane = lax.broadcasted_iota(jnp.int32, logits.shape, 1).astype(F32)
    valid = lane < n_experts
    lg = jnp.where(valid, logits, -jnp.inf)
    m1 = jnp.max(lg, axis=1, keepdims=True)
    i1 = jnp.min(jnp.where(lg == m1, lane, 1e9), axis=1, keepdims=True)
    lg2 = jnp.where(lane == i1, -jnp.inf, lg)
    m2 = jnp.max(lg2, axis=1, keepdims=True)
    i2 = jnp.min(jnp.where(lg2 == m2, lane, 1e9), axis=1, keepdims=True)
    e2 = jnp.exp(m2 - m1)
    w1 = 1.0 / (1.0 + e2)
    w2 = e2 / (1.0 + e2)
    comb_ref[...] = jnp.where(lane == i1, w1, 0.0) + jnp.where(lane == i2, w2, 0.0)


def _attnout_router(a, h, w_out, g, w_router, tm):
    t, d = h.shape
    n_experts = w_router.shape[1]
    assert n_experts <= LANES and MOE_TOP_K == 2
    wr = jnp.zeros((d, LANES), F32).at[:, :n_experts].set(w_router.astype(F32))
    tok = lambda n: pl.BlockSpec((tm, n), lambda i: (i, 0))
    full = lambda shape: pl.BlockSpec(shape, lambda i: (0,) * len(shape))
    return pl.pallas_call(
        functools.partial(_router_kernel, n_experts=n_experts),
        grid=(t // tm,),
        in_specs=[tok(d), tok(d), full(w_out.shape), full((1, d)), full(wr.shape)],
        out_specs=[tok(d), tok(LANES)],
        out_shape=[jax.ShapeDtypeStruct((t, d), F32), jax.ShapeDtypeStruct((t, LANES), F32)],
        compiler_params=_cparams("parallel"),
        name="attnout_router",
    )(a, h, w_out.astype(BF16), _row(g), wr)


def _moe_kernel(h_ref, comb_ref, g_ref, wg_ref, wu_ref, wd_ref, out_ref, xn_ref, acc_ref):
    e = pl.program_id(1)
    j = pl.program_id(2)

    @pl.when((e == 0) & (j == 0))
    def _():
        xn_ref[...] = _rms(h_ref[...], g_ref[...]).astype(BF16)
        acc_ref[...] = jnp.zeros_like(acc_ref)

    comb = comb_ref[...]
    lane = lax.broadcasted_iota(jnp.int32, comb.shape, 1)
    c_e = jnp.sum(jnp.where(lane == e, comb, 0.0), axis=1, keepdims=True)
    xn = xn_ref[...]
    gate = jnp.dot(xn, wg_ref[0], preferred_element_type=F32)
    up = jnp.dot(xn, wu_ref[0], preferred_element_type=F32)
    act = (_silu(gate) * up).astype(BF16)
    acc_ref[...] += c_e * jnp.dot(act, wd_ref[0], preferred_element_type=F32)

    @pl.when((e == pl.num_programs(1) - 1) & (j == pl.num_programs(2) - 1))
    def _():
        out_ref[...] = h_ref[...] + acc_ref[...]


def _moe(h, comb, g, w_gu, w_down, tm):
    t, d = h.shape
    n_experts, f, _ = w_down.shape
    tf = _ff_tile(f)
    nf = f // tf
    w_gu = w_gu.astype(BF16)
    return pl.pallas_call(
        _moe_kernel,
        grid=(t // tm, n_experts, nf),
        in_specs=[pl.BlockSpec((tm, d), lambda i, e, j: (i, 0)),
                  pl.BlockSpec((tm, LANES), lambda i, e, j: (i, 0)),
                  pl.BlockSpec((1, d), lambda i, e, j: (0, 0)),
                  pl.BlockSpec((1, d, tf), lambda i, e, j: (e, 0, j)),
                  pl.BlockSpec((1, d, tf), lambda i, e, j: (e, 0, nf + j)),
                  pl.BlockSpec((1, tf, d), lambda i, e, j: (e, j, 0))],
        out_specs=pl.BlockSpec((tm, d), lambda i, e, j: (i, 0)),
        out_shape=jax.ShapeDtypeStruct((t, d), F32),
        scratch_shapes=[pltpu.VMEM((tm, d), BF16), pltpu.VMEM((tm, d), F32)],
        compiler_params=_cparams("parallel", "arbitrary", "arbitrary"),
        name="moe",
    )(h, comb, _row(g), w_gu, w_gu, w_down.astype(BF16))


def _t5_bucket(dist):
    n = jnp.maximum(dist, 0)
    max_exact = REL_BUCKETS // 2
    large = max_exact + (jnp.log(jnp.maximum(n, 1).astype(F32) / max_exact)
                         / math.log(REL_MAX_DIST / max_exact) * (REL_BUCKETS - max_exact)).astype(jnp.int32)
    return jnp.where(n < max_exact, n, jnp.minimum(large, REL_BUCKETS - 1))


def _bias_by_distance(rel_bias, dist):
    return jnp.moveaxis(rel_bias.astype(F32)[_t5_bucket(dist)], -1, 0)


def _kmean_kernel(k_ref, out_ref):
    out_ref[0, 0] = jnp.mean(k_ref[0], axis=0, keepdims=True)


def _block_means(k, nb):
    b, s, d = k.shape
    out = pl.pallas_call(
        _kmean_kernel,
        grid=(b, nb),
        in_specs=[pl.BlockSpec((1, MOBA_BLOCK, d), lambda bi, n: (bi, n, 0))],
        out_specs=pl.BlockSpec((1, 1, 1, d), lambda bi, n: (bi, n, 0, 0)),
        out_shape=jax.ShapeDtypeStruct((b, nb, 1, d), F32),
        compiler_params=_cparams("parallel", "parallel"),
        name="moba_kmean",
    )(k)
    return out.reshape(b, nb, d)


def _top_blocks(gate, lane_f, n_valid, always):
    g = jnp.where(lane_f < n_valid, gate, -jnp.inf)
    sel = always
    for _ in range(MOBA_TOPK):
        m = jnp.max(g, axis=1, keepdims=True)
        idx = jnp.min(jnp.where(g == m, lane_f, 1e9), axis=1, keepdims=True)
        hit = lane_f == idx
        sel = sel | (hit & (lane_f < n_valid))
        g = jnp.where(hit, -jnp.inf, g)
    return sel


def _moba_prompt_kernel(q_ref, k_ref, v_ref, km_ref, bias_ref, out_ref, kaug_ref, vbf_ref, kmp_ref,
                        *, nb, n_delta, hd):
    qi = pl.program_id(2)
    blk = MOBA_BLOCK
    lane_i = lax.broadcasted_iota(jnp.int32, (blk, LANES), 1)

    @pl.when(qi == 0)
    def _():
        def fill(j, carry):
            rows = pl.ds(pl.multiple_of(j * blk, blk), blk)
            kaug_ref[rows, 0:hd] = k_ref[0, rows, :].astype(BF16)
            kaug_ref[rows, hd:hd + LANES] = jnp.where(lane_i == j, 1.0, 0.0).astype(BF16)
            vbf_ref[rows, :] = v_ref[0, rows, :].astype(BF16)
            return carry
        lax.fori_loop(0, nb, fill, 0)
        kmp_ref[...] = jnp.zeros_like(kmp_ref)
        kmp_ref[0:nb, :] = km_ref[0]

    q = q_ref[0]
    lane_f = lane_i.astype(F32)
    qi_f = qi.astype(F32)
    gate = lax.dot_general(q, kmp_ref[...], NT, precision=HI, preferred_element_type=F32)
    sel = _top_blocks(gate, lane_f, qi_f, lane_f == qi_f)
    q_aug = jnp.concatenate([(q * hd ** -0.5).astype(BF16), jnp.where(sel, 0.0, NEG).astype(BF16)], axis=1)

    def scores(j, tile):
        rows = pl.ds(pl.multiple_of(j * blk, blk), blk)
        s = lax.dot_general(q_aug, kaug_ref[rows, :], NT, preferred_element_type=F32)
        return s + bias_ref[0, tile], rows

    s, rows = scores(qi, 0)
    r_i = lax.broadcasted_iota(jnp.int32, (blk, blk), 0)
    c_i = lax.broadcasted_iota(jnp.int32, (blk, blk), 1)
    s = jnp.where(r_i >= c_i, s, NEG)
    m0 = jnp.max(s, axis=1, keepdims=True)
    p = jnp.exp(s - m0)
    l0 = jnp.sum(p, axis=1, keepdims=True)
    acc0 = jnp.dot(p.astype(BF16), vbf_ref[rows, :], preferred_element_type=F32)

    def past(j, carry):
        m, l, acc = carry
        s, rows = scores(j, jnp.minimum(qi - j, n_delta - 1))
        m_new = jnp.maximum(m, jnp.max(s, axis=1, keepdims=True))
        alpha = jnp.exp(m - m_new)
        p = jnp.exp(s - m_new)
        l = alpha * l + jnp.sum(p, axis=1, keepdims=True)
        acc = alpha * acc + jnp.dot(p.astype(BF16), vbf_ref[rows, :], preferred_element_type=F32)
        return m_new, l, acc

    _, l, acc = lax.fori_loop(0, qi, past, (m0, l0, acc0))
    out_ref[0] = acc / l


def _moba_prompt(q, k, v, rel_bias):
    b, s, d = q.shape
    hd = d // MOBA_HEADS
    blk = MOBA_BLOCK
    assert s % blk == 0 and hd == LANES
    nb = s // blk
    assert nb <= LANES
    n_delta = min(nb, (REL_MAX_DIST + blk - 1) // blk + 2)
    delta = jnp.arange(n_delta)[:, None, None] * blk + jnp.arange(blk)[None, :, None] - jnp.arange(blk)[None, None, :]
    bias = _bias_by_distance(rel_bias, delta)
    kmean = _block_means(k, nb)
    qspec = pl.BlockSpec((1, blk, hd), lambda bi, hi, qi: (bi, qi, hi))
    kvspec = pl.BlockSpec((1, s, hd), lambda bi, hi, qi: (bi, 0, hi))
    return pl.pallas_call(
        functools.partial(_moba_prompt_kernel, nb=nb, n_delta=n_delta, hd=hd),
        grid=(b, MOBA_HEADS, nb),
        in_specs=[qspec, kvspec, kvspec,
                  pl.BlockSpec((1, nb, hd), lambda bi, hi, qi: (bi, 0, hi)),
                  pl.BlockSpec((1, n_delta, blk, blk), lambda bi, hi, qi: (hi, 0, 0, 0))],
        out_specs=qspec,
        out_shape=jax.ShapeDtypeStruct((b, s, d), F32),
        scratch_shapes=[pltpu.VMEM((s, hd + LANES), BF16), pltpu.VMEM((s, hd), BF16), pltpu.VMEM((LANES, hd), F32)],
        compiler_params=_cparams("parallel", "arbitrary", "arbitrary"),
        name="moba_prompt",
    )(q, k, v, kmean, bias)


def _page_block_mean_kernel(pt_ref, *refs):
    out_ref = refs[-1]
    total = refs[0][0].astype(F32).mean(axis=0)
    for r in refs[1:-1]:
        total = total + r[0].astype(F32).mean(axis=0)
    out_ref[0, 0] = total / (len(refs) - 1)


def _page_block_means(cache_k, page_table, nb, ppb):
    n_phys, page, h, hd = cache_k.shape
    db = page_table.shape[0]

    def page_spec(p):
        return pl.BlockSpec((1, page, h, hd), lambda bi, n, pt: (pt[bi, n * ppb + p], 0, 0, 0))

    return pl.pallas_call(
        _page_block_mean_kernel,
        grid_spec=pltpu.PrefetchScalarGridSpec(
            num_scalar_prefetch=1, grid=(db, nb),
            in_specs=[page_spec(p) for p in range(ppb)],
            out_specs=pl.BlockSpec((1, 1, h, hd), lambda bi, n, pt: (bi, n, 0, 0))),
        out_shape=jax.ShapeDtypeStruct((db, nb, h, hd), F32),
        compiler_params=_cparams("arbitrary", "arbitrary"),
        name="moba_page_means",
    )(page_table, *([cache_k] * ppb))


def _sample_topk_kernel(q_ref, km_ref, idx_ref, kmp_ref, *, nb, hd):
    lq = q_ref.shape[1]
    lane_f = lax.broadcasted_iota(jnp.int32, (lq, LANES), 1).astype(F32)
    kmp_ref[...] = jnp.zeros_like(kmp_ref)
    for h in range(MOBA_HEADS):
        sl = slice(h * hd, (h + 1) * hd)
        kmp_ref[0:nb, :] = km_ref[0, :, sl]
        g = lax.dot_general(q_ref[0, :, sl], kmp_ref[...], NT, precision=HI, preferred_element_type=F32)
        g = jnp.where(lane_f < nb, g, -jnp.inf)
        out = jnp.zeros((lq, LANES), F32)
        for r in range(MOBA_TOPK):
            m = jnp.max(g, axis=1, keepdims=True)
            idx = jnp.min(jnp.where(g == m, lane_f, 1e9), axis=1, keepdims=True)
            out = jnp.where(lane_f == r, idx, out)
            g = jnp.where(lane_f == idx, -jnp.inf, g)
        idx_ref[0, h] = out.astype(jnp.int32)


def _sample_topk(q, kmean, nb):
    db, lq, d = q.shape
    hd = d // MOBA_HEADS
    out = pl.pallas_call(
        functools.partial(_sample_topk_kernel, nb=nb, hd=hd),
        grid=(db,),
        in_specs=[pl.BlockSpec((1, lq, d), lambda bi: (bi, 0, 0)),
                  pl.BlockSpec((1, nb, d), lambda bi: (bi, 0, 0))],
        out_specs=pl.BlockSpec((1, MOBA_HEADS, lq, LANES), lambda bi: (bi, 0, 0, 0)),
        out_shape=jax.ShapeDtypeStruct((db, MOBA_HEADS, lq, LANES), jnp.int32),
        scratch_shapes=[pltpu.VMEM((LANES, hd), F32)],
        compiler_params=_cparams("parallel"),
        name="moba_sample_topk",
    )(q, kmean)
    return out[..., :MOBA_TOPK]


def _moba_sample_kernel(pt_ref, idx_ref, q_ref, kn_ref, vn_ref, bsel_ref, bown_ref, ck_ref, cv_ref, out_ref,
                        kbuf, vbuf, sem, *, lq, ppb, page, hd):
    bi = pl.program_id(0)
    hi = pl.program_id(1)
    n_heads = pl.num_programs(1)
    step = bi * n_heads + hi
    n_steps = pl.num_programs(0) * n_heads
    n_pick = lq * MOBA_TOPK
    blk = ppb * page

    def copies(b2, h2, slot, i):
        block = idx_ref[(b2 * n_heads + h2) * n_pick + i]
        out = []
        for p in range(ppb):
            phys = pt_ref[b2, block * ppb + p]
            out.append(pltpu.make_async_copy(ck_ref.at[phys, :, h2, :], kbuf.at[slot, i * ppb + p], sem.at[0, slot]))
            out.append(pltpu.make_async_copy(cv_ref.at[phys, :, h2, :], vbuf.at[slot, i * ppb + p], sem.at[1, slot]))
        return out

    def issue(step2, slot):
        b2 = step2 // n_heads
        h2 = step2 % n_heads
        for i in range(n_pick):
            for c in copies(b2, h2, slot, i):
                c.start()

    slot = step % 2

    @pl.when(step == 0)
    def _():
        issue(step, slot)

    @pl.when(step + 1 < n_steps)
    def _():
        issue(step + 1, 1 - slot)

    for i in range(n_pick):
        for c in copies(bi, hi, slot, i):
            c.wait()

    scale = hd ** -0.5
    qb = (q_ref[0] * scale).astype(BF16)
    row = lax.broadcasted_iota(jnp.int32, (lq, 1), 0)
    r_i = lax.broadcasted_iota(jnp.int32, (lq, lq), 0)
    c_i = lax.broadcasted_iota(jnp.int32, (lq, lq), 1)
    s_own = lax.dot_general(qb, kn_ref[0].astype(BF16), NT, preferred_element_type=F32) + bown_ref[0]
    s_own = jnp.where(r_i >= c_i, s_own, NEG)
    m = jnp.max(s_own, axis=1, keepdims=True)
    p_own = jnp.exp(s_own - m)
    l = jnp.sum(p_own, axis=1, keepdims=True)
    acc = jnp.dot(p_own.astype(BF16), vn_ref[0].astype(BF16), preferred_element_type=F32)
    n_rows = MOBA_TOPK * blk
    for qq in range(lq):
        first = qq * MOBA_TOPK * ppb
        kq = kbuf[slot, first:first + MOBA_TOPK * ppb].reshape(n_rows, hd).astype(BF16)
        vq = vbuf[slot, first:first + MOBA_TOPK * ppb].reshape(n_rows, hd).astype(BF16)
        bias = jnp.concatenate(
            [bsel_ref[0, qq, pl.ds(idx_ref[(bi * n_heads + hi) * n_pick + qq * MOBA_TOPK + t], 1), :]
             for t in range(MOBA_TOPK)], axis=1)
        s = lax.dot_general(qb, kq, NT, preferred_element_type=F32) + bias
        mine = row == qq
        m_new = jnp.where(mine, jnp.maximum(m, jnp.max(s, axis=1, keepdims=True)), m)
        alpha = jnp.exp(m - m_new)
        p = jnp.where(mine, jnp.exp(s - m_new), 0.0)
        l = alpha * l + jnp.sum(p, axis=1, keepdims=True)
        acc = alpha * acc + jnp.dot(p.astype(BF16), vq, preferred_element_type=F32)
        m = m_new
    out_ref[0] = acc / l


def _moba_sample(q, k_new, v_new, cache_k, cache_v, page_table, rel_bias):
    db, lq, d = q.shape
    n_phys, page, n_heads, hd = cache_k.shape
    assert n_heads == MOBA_HEADS and hd * n_heads == d
    n_pages = page_table.shape[1]
    past = n_pages * page
    ppb = MOBA_BLOCK // page
    nb = past // MOBA_BLOCK
    assert past % MOBA_BLOCK == 0, "new tokens must start a fresh MoBA block"
    assert MOBA_TOPK <= nb <= LANES and lq <= MOBA_BLOCK
    blk = MOBA_BLOCK

    kmean = _page_block_means(cache_k, page_table, nb, ppb).reshape(db, nb, d)
    idx = _sample_topk(q, kmean, nb)

    qpos = past + jnp.arange(lq)
    d_sel = qpos[:, None, None] - (jnp.arange(nb)[None, :, None] * blk + jnp.arange(blk)[None, None, :])
    bias_sel = _bias_by_distance(rel_bias, d_sel)
    bias_own = _bias_by_distance(rel_bias, qpos[:, None] - qpos[None, :])

    tok = pl.BlockSpec((1, lq, hd), lambda bi, hi, pt, ix: (bi, 0, hi))
    n_slots = lq * MOBA_TOPK * ppb
    return pl.pallas_call(
        functools.partial(_moba_sample_kernel, lq=lq, ppb=ppb, page=page, hd=hd),
        grid_spec=pltpu.PrefetchScalarGridSpec(
            num_scalar_prefetch=2, grid=(db, n_heads),
            in_specs=[tok, tok, tok,
                      pl.BlockSpec((1, lq, nb, blk), lambda bi, hi, pt, ix: (hi, 0, 0, 0)),
                      pl.BlockSpec((1, lq, lq), lambda bi, hi, pt, ix: (hi, 0, 0)),
                      pl.BlockSpec(memory_space=pl.ANY),
                      pl.BlockSpec(memory_space=pl.ANY)],
            out_specs=tok,
            scratch_shapes=[pltpu.VMEM((2, n_slots, page, hd), cache_k.dtype),
                            pltpu.VMEM((2, n_slots, page, hd), cache_v.dtype),
                            pltpu.SemaphoreType.DMA((2, 2))]),
        out_shape=jax.ShapeDtypeStruct((db, lq, d), F32),
        compiler_params=_cparams("arbitrary", "arbitrary"),
        name="moba_sample",
    )(page_table, idx.reshape(-1), q, k_new, v_new, bias_sel, bias_own, cache_k, cache_v)


def _trunk(x, p, s0, attend, w):
    b, l, d = x.shape
    t = b * l
    tm = _token_tile(t, 512)
    hd = d // MOBA_HEADS
    h = x.reshape(t, d)

    q, k, v, r, la = _gla_in(h, w["g_mix"][0], w["w_a_in"][0], w["w_a_gate2"][0], w["b_a_gate"][0], tm)
    seq = lambda a: a.reshape(b, l, a.shape[1])
    o, s_new = _gla_chunked(seq(q), seq(k), seq(v), seq(la), s0)
    h = _gla_out(o.reshape(t, -1), r, h, w["g_a_onorm"][0], w["w_a_out"][0], tm)
    h = _ffn(h, w["g_ffn"][0], w["w_ffn_gu"][0], w["w_ffn_down"][0], tm)
    h, k_sh, v_sh, q_b = _ple(
        h, p[0].reshape(t, -1), w["g_ple"][0], w["w_ple_gate"][0], w["w_ple_proj"][0],
        [(w["g_kv"], w["w_kv"][:, :d]), (w["g_kv"], w["w_kv"][:, d:]), (w["g_mix"][1], w["w_b_q"][0])], tm)

    a = attend(seq(q_b), seq(k_sh), seq(v_sh))
    h, comb = _attnout_router(a.reshape(t, d), h, w["w_b_out"][0], w["g_ffn"][1], w["w_router"][0], tm)
    h = _moe(h, comb, w["g_ffn"][1], w["w_moe_gu"][0], w["w_moe_down"][0], tm)
    _, y = _ple(h, p[1].reshape(t, -1), w["g_ple"][1], w["w_ple_gate"][1], w["w_ple_proj"][1],
                [(w["g_final"], None)], tm)
    heads = lambda a: a.reshape(b, l, MOBA_HEADS, hd)
    return y.reshape(b, l, d), s_new[None], heads(k_sh), heads(v_sh)


def kernel(x_prompt, x_sample, state_gla, cache_k, cache_v, page_table, p_prompt, p_sample, g_mix, g_ffn, g_ple,
           w_a_in, w_a_gate2, b_a_gate, g_a_onorm, w_a_out, g_kv, w_kv, w_b_q, w_b_out, rel_bias, w_ffn_gu,
           w_ffn_down, w_router, w_moe_gu, w_moe_down, w_ple_proj, w_ple_gate, g_final):
    assert g_mix.shape[0] == 2 and state_gla.shape[0] == 1, "layer 0 is GLA, layer 1 is MoBA"
    w = dict(g_mix=g_mix, g_ffn=g_ffn, g_ple=g_ple, w_a_in=w_a_in, w_a_gate2=w_a_gate2, b_a_gate=b_a_gate,
             g_a_onorm=g_a_onorm, w_a_out=w_a_out, g_kv=g_kv, w_kv=w_kv, w_b_q=w_b_q, w_b_out=w_b_out,
             w_ffn_gu=w_ffn_gu, w_ffn_down=w_ffn_down, w_router=w_router, w_moe_gu=w_moe_gu,
             w_moe_down=w_moe_down, w_ple_proj=w_ple_proj, w_ple_gate=w_ple_gate, g_final=g_final)
    bp = x_prompt.shape[0]
    zeros = jnp.zeros((bp,) + state_gla.shape[2:], F32)
    y_p, st_p, k_p, v_p = _trunk(x_prompt, p_prompt, zeros,
                                 lambda q, k, v: _moba_prompt(q, k, v, rel_bias), w)
    y_s, st_s, k_s, v_s = _trunk(x_sample, p_sample, state_gla[0],
                                 lambda q, k, v: _moba_sample(q, k, v, cache_k, cache_v, page_table, rel_bias), w)
    return (y_p, y_s, st_p, st_s, k_p, v_p, k_s, v_s)
```

```python
import functools
import math

import jax
import jax.numpy as jnp
from jax import lax
from jax.experimental import pallas as pl
from jax.experimental.pallas import tpu as pltpu

F32 = jnp.float32
BF16 = jnp.bfloat16
HI = lax.Precision.HIGHEST

GLA_HEADS = 4
GLA_GATE_NORM = 16.0
GLA_CHUNK = 64
MOBA_HEADS = 8
MOBA_BLOCK = 256
MOBA_TOPK = 3
REL_BUCKETS = 32
REL_MAX_DIST = 4096
MOE_TOP_K = 2
RMS_EPS = 1e-6

LANES = 128
VMEM_LIMIT = 56 * 1024 * 1024
NEG = -1e30
LOG2E = math.log2(math.e)

NT = (((1,), (1,)), ((), ()))
TN = (((0,), (0,)), ((), ()))


def _cparams(*sem):
    return pltpu.CompilerParams(dimension_semantics=sem, vmem_limit_bytes=VMEM_LIMIT)


def _token_tile(t, cap):
    tm = cap
    while t % tm:
        tm //= 2
    return tm


def _rms(x, g):
    return x * lax.rsqrt(jnp.mean(x * x, axis=-1, keepdims=True) + RMS_EPS) * g


def _sigmoid(x):
    return 1.0 / (1.0 + jnp.exp(-x))


def _silu(x):
    return x * _sigmoid(x)


def _row(v):
    return v.reshape(1, -1).astype(F32)


def _gla_in_kernel(h_ref, g_ref, w_ref, wlr_ref, wg2_ref, bg_ref,
                   q_ref, k_ref, v_ref, r_ref, la_ref, *, qk, vd, qscale):
    xn = _rms(h_ref[...], g_ref[...]).astype(BF16)
    proj = jnp.dot(xn, w_ref[...], preferred_element_type=F32)
    q_ref[...] = proj[:, :qk] * qscale
    k_ref[...] = proj[:, qk:2 * qk]
    v_ref[...] = proj[:, 2 * qk:2 * qk + vd]
    r_ref[...] = proj[:, 2 * qk + vd:]
    glr = jnp.dot(xn, wlr_ref[...], preferred_element_type=F32)
    z = jnp.dot(glr, wg2_ref[...], precision=HI, preferred_element_type=F32) + bg_ref[...]
    log_sig = jnp.minimum(z, 0.0) - jnp.log1p(jnp.exp(-jnp.abs(z)))
    la_ref[...] = log_sig / GLA_GATE_NORM


def _gla_in(h, g, w_in, w_gate2, b_gate, tm):
    t, d = h.shape
    qk = w_gate2.shape[1]
    rank = w_gate2.shape[0]
    vd = (w_in.shape[1] - 2 * qk - rank) // 2
    dk = qk // GLA_HEADS
    n_main = 2 * qk + 2 * vd
    w_main = w_in[:, :n_main].astype(BF16)
    w_lr = w_in[:, n_main:].astype(BF16)
    tok = lambda n: pl.BlockSpec((tm, n), lambda i: (i, 0))
    full = lambda a: pl.BlockSpec(a.shape, lambda i: (0,) * a.ndim)
    args = (h, _row(g), w_main, w_lr, w_gate2.astype(F32), _row(b_gate))
    return pl.pallas_call(
        functools.partial(_gla_in_kernel, qk=qk, vd=vd, qscale=dk ** -0.5),
        grid=(t // tm,),
        in_specs=[tok(d)] + [full(a) for a in args[1:]],
        out_specs=[tok(qk), tok(qk), tok(vd), tok(vd), tok(qk)],
        out_shape=[jax.ShapeDtypeStruct((t, n), F32) for n in (qk, qk, vd, vd, qk)],
        compiler_params=_cparams("parallel"),
        name="gla_in",
    )(*args)


def _gla_kernel(q_ref, k_ref, v_ref, la_ref, s0_ref, o_ref, sout_ref, st_ref, *, chunk, n_chunks, mm_dtype):
    t = pl.program_id(2)

    @pl.when(t == 0)
    def _():
        st_ref[...] = s0_ref[0, 0].T

    rows = lax.broadcasted_iota(jnp.int32, (chunk, chunk), 0)
    cols = lax.broadcasted_iota(jnp.int32, (chunk, chunk), 1)
    tri = rows >= cols
    tri_f = tri.astype(F32)
    for c in range(n_chunks):
        sl = slice(c * chunk, (c + 1) * chunk)
        b = jnp.dot(tri_f, la_ref[0, sl, :], precision=HI, preferred_element_type=F32)
        b_last = b[chunk - 1:chunk, :]
        q = q_ref[0, sl, :]
        k = k_ref[0, sl, :]
        v = v_ref[0, sl, :].astype(mm_dtype)
        q_dec = (q * jnp.exp(b)).astype(mm_dtype)
        k_inv = (k * jnp.exp(-b)).astype(mm_dtype)
        k_end = (k * jnp.exp(b_last - b)).astype(mm_dtype)
        attn = lax.dot_general(q_dec, k_inv, NT, preferred_element_type=F32)
        attn = jnp.where(tri, attn, 0.0).astype(mm_dtype)
        st = st_ref[...]
        o = jnp.dot(attn, v, preferred_element_type=F32)
        o = o + lax.dot_general(q_dec, st.astype(mm_dtype), NT, preferred_element_type=F32)
        o_ref[0, sl, :] = o
        st_ref[...] = st * jnp.exp(b_last) + lax.dot_general(v, k_end, TN, preferred_element_type=F32)

    @pl.when(t == pl.num_programs(2) - 1)
    def _():
        sout_ref[0, 0] = st_ref[...].T


def _gla_chunked(q, k, v, la, s0):
    b, l, qk = q.shape
    dk = qk // GLA_HEADS
    dv = v.shape[2] // GLA_HEADS
    chunk = math.gcd(l, GLA_CHUNK)
    tl = _token_tile(l, 512)
    n_chunks = tl // chunk
    mm_dtype = BF16 if chunk % 16 == 0 else F32
    kspec = pl.BlockSpec((1, tl, dk), lambda bi, hi, ti: (bi, ti, hi))
    vspec = pl.BlockSpec((1, tl, dv), lambda bi, hi, ti: (bi, ti, hi))
    sspec = pl.BlockSpec((1, 1, dk, dv), lambda bi, hi, ti: (bi, hi, 0, 0))
    return pl.pallas_call(
        functools.partial(_gla_kernel, chunk=chunk, n_chunks=n_chunks, mm_dtype=mm_dtype),
        grid=(b, GLA_HEADS, l // tl),
        in_specs=[kspec, kspec, vspec, kspec, sspec],
        out_specs=[vspec, sspec],
        out_shape=[jax.ShapeDtypeStruct(v.shape, F32), jax.ShapeDtypeStruct(s0.shape, F32)],
        scratch_shapes=[pltpu.VMEM((dv, dk), F32)],
        compiler_params=_cparams("parallel", "parallel", "arbitrary"),
        name="gla_chunk",
    )(q, k, v, la, s0.astype(F32))


def _gla_out_kernel(o_ref, r_ref, h_ref, g_ref, w_ref, out_ref, *, dv):
    g = g_ref[...]
    parts = []
    for hd in range(GLA_HEADS):
        sl = slice(hd * dv, (hd + 1) * dv)
        parts.append(_rms(o_ref[:, sl], g) * _silu(r_ref[:, sl]))
    y = jnp.concatenate(parts, axis=1).astype(BF16)
    out_ref[...] = h_ref[...] + jnp.dot(y, w_ref[...], preferred_element_type=F32)


def _gla_out(o, r, h, g_onorm, w_out, tm):
    t, d = h.shape
    vd = o.shape[1]
    tok = lambda n: pl.BlockSpec((tm, n), lambda i: (i, 0))
    full = lambda shape: pl.BlockSpec(shape, lambda i: (0,) * len(shape))
    return pl.pallas_call(
        functools.partial(_gla_out_kernel, dv=vd // GLA_HEADS),
        grid=(t // tm,),
        in_specs=[tok(vd), tok(vd), tok(d), full((1, vd // GLA_HEADS)), full(w_out.shape)],
        out_specs=tok(d),
        out_shape=jax.ShapeDtypeStruct((t, d), F32),
        compiler_params=_cparams("parallel"),
        name="gla_out",
    )(o, r, h, _row(g_onorm), w_out.astype(BF16))


def _ffn_kernel(h_ref, g_ref, wg_ref, wu_ref, wd_ref, out_ref, xn_ref, acc_ref):
    j = pl.program_id(1)

    @pl.when(j == 0)
    def _():
        xn_ref[...] = _rms(h_ref[...], g_ref[...]).astype(BF16)
        acc_ref[...] = jnp.zeros_like(acc_ref)

    xn = xn_ref[...]
    gate = jnp.dot(xn, wg_ref[...], preferred_element_type=F32)
    up = jnp.dot(xn, wu_ref[...], preferred_element_type=F32)
    act = (_silu(gate) * up).astype(BF16)
    acc_ref[...] += jnp.dot(act, wd_ref[...], preferred_element_type=F32)

    @pl.when(j == pl.num_programs(1) - 1)
    def _():
        out_ref[...] = h_ref[...] + acc_ref[...]


def _ff_tile(f):
    for nf in (1, 2, 4, 11, 22):
        if f % nf == 0 and (f // nf) % LANES == 0 and f // nf <= 1536:
            return f // nf
    return f


def _ffn(h, g, w_gu, w_down, tm):
    t, d = h.shape
    f = w_down.shape[0]
    tf = _ff_tile(f)
    nf = f // tf
    w_gu = w_gu.astype(BF16)
    return pl.pallas_call(
        _ffn_kernel,
        grid=(t // tm, nf),
        in_specs=[pl.BlockSpec((tm, d), lambda i, j: (i, 0)),
                  pl.BlockSpec((1, d), lambda i, j: (0, 0)),
                  pl.BlockSpec((d, tf), lambda i, j: (0, j)),
                  pl.BlockSpec((d, tf), lambda i, j: (0, nf + j)),
                  pl.BlockSpec((tf, d), lambda i, j: (j, 0))],
        out_specs=pl.BlockSpec((tm, d), lambda i, j: (i, 0)),
        out_shape=jax.ShapeDtypeStruct((t, d), F32),
        scratch_shapes=[pltpu.VMEM((tm, d), BF16), pltpu.VMEM((tm, d), F32)],
        compiler_params=_cparams("parallel", "arbitrary"),
        name="ffn",
    )(h, _row(g), w_gu, w_gu, w_down.astype(BF16))


def _ple_kernel(*refs, heads):
    h_ref, p_ref, g_ref, wgate_ref, wproj_ref = refs[:5]
    n_in = sum(2 if has_w else 1 for has_w, _ in heads)
    head_refs = refs[5:5 + n_in]
    out_refs = refs[5 + n_in:]
    h = h_ref[...]
    gate = _sigmoid(jnp.dot(_rms(h, g_ref[...]).astype(BF16), wgate_ref[...], preferred_element_type=F32))
    emb = jnp.dot(p_ref[...].astype(BF16), wproj_ref[...], preferred_element_type=F32)
    h = h + gate * emb
    out_refs[0][...] = h
    i_in, i_out = 0, 1
    for has_w, copy16 in heads:
        y = _rms(h, head_refs[i_in][...])
        i_in += 1
        if has_w:
            y = jnp.dot(y.astype(BF16), head_refs[i_in][...], preferred_element_type=F32)
            i_in += 1
        out_refs[i_out][...] = y
        i_out += 1
        if copy16:
            out_refs[i_out][...] = y.astype(BF16)
            i_out += 1


def _ple(h, p, g, w_gate, w_proj, heads, tm):
    t, d = h.shape
    tok = lambda n: pl.BlockSpec((tm, n), lambda i: (i, 0))
    full = lambda a: pl.BlockSpec(a.shape, lambda i: (0,) * a.ndim)
    args = [h, p, _row(g), w_gate.astype(BF16), w_proj.astype(BF16)]
    flags = []
    outs = [(d, F32)]
    for gain, w, copy16 in heads:
        args.append(_row(gain))
        if w is not None:
            args.append(w.astype(BF16))
        n = d if w is None else w.shape[1]
        outs.append((n, F32))
        if copy16:
            outs.append((n, BF16))
        flags.append((w is not None, copy16))
    return pl.pallas_call(
        functools.partial(_ple_kernel, heads=tuple(flags)),
        grid=(t // tm,),
        in_specs=[tok(d), tok(p.shape[1])] + [full(a) for a in args[2:]],
        out_specs=[tok(n) for n, _ in outs],
        out_shape=[jax.ShapeDtypeStruct((t, n), dt) for n, dt in outs],
        compiler_params=_cparams("parallel"),
        name="ple",
    )(*args)


def _router_kernel(a_ref, h_ref, wo_ref, g_ref, wr_ref, out_ref, comb_ref, *, n_experts):
    h = h_ref[...] + jnp.dot(a_ref[...].astype(BF16), wo_ref[...], preferred_element_type=F32)
    out_ref[...] = h
    logits = jnp.dot(_rms(h, g_ref[...]), wr_ref[...], precision=HI, preferred_element_type=F32)
    lane = lax.broadcasted_iota(jnp.int32, logits.shape, 1).astype(F32)
    valid = lane < n_experts
    lg = jnp.where(valid, logits, -jnp.inf)
    m1 = jnp.max(lg, axis=1, keepdims=True)
    i1 = jnp.min(jnp.where(lg == m1, lane, 1e9), axis=1, keepdims=True)
    lg2 = jnp.where(lane == i1, -jnp.inf, lg)
    m2 = jnp.max(lg2, axis=1, keepdims=True)
    i2 = jnp.min(jnp.where(lg2 == m2, lane, 1e9), axis=1, keepdims=True)
    e2 = jnp.exp(m2 - m1)
    w1 = 1.0 / (1.0 + e2)
    w2 = e2 / (1.0 + e2)
    comb_ref[...] = jnp.where(lane == i1, w1, 0.0) + jnp.where(lane == i2, w2, 0.0)


def _attnout_router(a, h, w_out, g, w_router, tm):
    t, d = h.shape
    n_experts = w_router.shape[1]
    assert n_experts <= LANES and MOE_TOP_K == 2
    wr = jnp.zeros((d, LANES), F32).at[:, :n_experts].set(w_router.astype(F32))
    tok = lambda n: pl.BlockSpec((tm, n), lambda i: (i, 0))
    full = lambda shape: pl.BlockSpec(shape, lambda i: (0,) * len(shape))
    return pl.pallas_call(
        functools.partial(_router_kernel, n_experts=n_experts),
        grid=(t // tm,),
        in_specs=[tok(d), tok(d), full(w_out.shape), full((1, d)), full(wr.shape)],
        out_specs=[tok(d), tok(LANES)],
        out_shape=[jax.ShapeDtypeStruct((t, d), F32), jax.ShapeDtypeStruct((t, LANES), F32)],
        compiler_params=_cparams("parallel"),
        name="attnout_router",
    )(a, h, w_out.astype(BF16), _row(g), wr)


def _moe_kernel(h_ref, comb_ref, g_ref, wg_ref, wu_ref, wd_ref, out_ref, xn_ref, acc_ref):
    e = pl.program_id(1)
    j = pl.program_id(2)

    @pl.when((e == 0) & (j == 0))
    def _():
        xn_ref[...] = _rms(h_ref[...], g_ref[...]).astype(BF16)
        acc_ref[...] = jnp.zeros_like(acc_ref)

    comb = comb_ref[...]
    lane = lax.broadcasted_iota(jnp.int32, comb.shape, 1)
    c_e = jnp.sum(jnp.where(lane == e, comb, 0.0), axis=1, keepdims=True)
    xn = xn_ref[...]
    gate = jnp.dot(xn, wg_ref[0], preferred_element_type=F32)
    up = jnp.dot(xn, wu_ref[0], preferred_element_type=F32)
    act = (_silu(gate) * up).astype(BF16)
    acc_ref[...] += c_e * jnp.dot(act, wd_ref[0], preferred_element_type=F32)

    @pl.when((e == pl.num_programs(1) - 1) & (j == pl.num_programs(2) - 1))
    def _():
        out_ref[...] = h_ref[...] + acc_ref[...]


def _moe(h, comb, g, w_gu, w_down, tm):
    t, d = h.shape
    n_experts, f, _ = w_down.shape
    tf = _ff_tile(f)
    nf = f // tf
    w_gu = w_gu.astype(BF16)
    return pl.pallas_call(
        _moe_kernel,
        grid=(t // tm, n_experts, nf),
        in_specs=[pl.BlockSpec((tm, d), lambda i, e, j: (i, 0)),
                  pl.BlockSpec((tm, LANES), lambda i, e, j: (i, 0)),
                  pl.BlockSpec((1, d), lambda i, e, j: (0, 0)),
                  pl.BlockSpec((1, d, tf), lambda i, e, j: (e, 0, j)),
                  pl.BlockSpec((1, d, tf), lambda i, e, j: (e, 0, nf + j)),
                  pl.BlockSpec((1, tf, d), lambda i, e, j: (e, j, 0))],
        out_specs=pl.BlockSpec((tm, d), lambda i, e, j: (i, 0)),
        out_shape=jax.ShapeDtypeStruct((t, d), F32),
        scratch_shapes=[pltpu.VMEM((tm, d), BF16), pltpu.VMEM((tm, d), F32)],
        compiler_params=_cparams("parallel", "arbitrary", "arbitrary"),
        name="moe",
    )(h, comb, _row(g), w_gu, w_gu, w_down.astype(BF16))


def _t5_bucket(dist):
    n = jnp.maximum(dist, 0)
    max_exact = REL_BUCKETS // 2
    large = max_exact + (jnp.log(jnp.maximum(n, 1).astype(F32) / max_exact)
                         / math.log(REL_MAX_DIST / max_exact) * (REL_BUCKETS - max_exact)).astype(jnp.int32)
    return jnp.where(n < max_exact, n, jnp.minimum(large, REL_BUCKETS - 1))


def _bias_table(rel_bias, n):
    return rel_bias.astype(F32)[_t5_bucket(jnp.arange(n))].T


def _toeplitz_tiles(strips, blk):
    lead = strips.shape[:-1]
    w = 2 * blk + 1
    flat = jnp.broadcast_to(strips[..., None, :], lead + (blk, w)).reshape(lead + (blk * w,))
    return flat[..., :blk * (w - 1)].reshape(lead + (blk, w - 1))[..., blk:]


def _kmean_kernel(k_ref, out_ref):
    out_ref[0, 0] = jnp.mean(k_ref[0], axis=0, keepdims=True)


def _block_means(k, nb):
    b, s, d = k.shape
    out = pl.pallas_call(
        _kmean_kernel,
        grid=(b, nb),
        in_specs=[pl.BlockSpec((1, MOBA_BLOCK, d), lambda bi, n: (bi, n, 0))],
        out_specs=pl.BlockSpec((1, 1, 1, d), lambda bi, n: (bi, n, 0, 0)),
        out_shape=jax.ShapeDtypeStruct((b, nb, 1, d), F32),
        compiler_params=_cparams("parallel", "parallel"),
        name="moba_kmean",
    )(k)
    return out.reshape(b, nb, d)


def _moba_prompt_kernel(q_ref, k_ref, v_ref, km_ref, bias_ref, out_ref, kaug_ref, vt_ref, kmp_ref,
                        *, nb, n_delta, hd, group, kb):
    qi = pl.program_id(2)
    blk = MOBA_BLOCK
    span = kb * blk
    groups = range(group)

    @pl.when(qi == 0)
    def _():
        lane_i = lax.broadcasted_iota(jnp.int32, (blk, LANES), 1)

        def fill(sp, carry):
            for b in range(kb):
                j = sp * kb + b
                rows = pl.ds(pl.multiple_of(j * blk, blk), blk)
                onehot = jnp.where(lane_i == j, 1.0, 0.0).astype(BF16)
                for g in groups:
                    kaug_ref[g, rows, 0:hd] = k_ref[g, rows, :]
                    kaug_ref[g, rows, hd:hd + LANES] = onehot
                    vt_ref[g, sp, :, b * blk:(b + 1) * blk] = v_ref[g, rows, :].astype(F32).T.astype(BF16)
            return carry
        lax.fori_loop(0, nb // kb, fill, 0)
        kmp_ref[...] = jnp.zeros_like(kmp_ref)
        for g in groups:
            kmp_ref[g, 0:nb, :] = km_ref[g]

    blk_i = lax.broadcasted_iota(jnp.int32, (LANES, blk), 0)
    blk_f = blk_i.astype(F32)
    qscale = hd ** -0.5 * LOG2E

    q_aug = []
    for g in groups:
        q = q_ref[g]
        gate = lax.dot_general(kmp_ref[g], q, NT, precision=HI, preferred_element_type=F32)
        gate = jnp.where(blk_i < qi, gate, -jnp.inf)
        sel = blk_i == qi
        for _ in range(MOBA_TOPK):
            best = jnp.max(gate, axis=0, keepdims=True)
            idx = jnp.min(jnp.where(gate == best, blk_f, 1e9), axis=0, keepdims=True)
            hit = blk_f == idx
            sel = sel | (hit & (blk_i < qi))
            gate = jnp.where(hit, -jnp.inf, gate)
        sel_bias = jnp.where(sel, 0.0, NEG).T
        q_aug.append(jnp.concatenate([(q * qscale).astype(BF16), sel_bias.astype(BF16)], axis=1))

    def span_scores(sp):
        rows = pl.ds(pl.multiple_of(sp * span, span), span)
        bias = jnp.concatenate(
            [bias_ref[0, jnp.clip(qi - (sp * kb + b), 0, n_delta - 1)] for b in range(kb)], axis=0)
        return [lax.dot_general(kaug_ref[g, rows, :], q_aug[g], NT, preferred_element_type=F32) + bias
                for g in groups]

    last = qi // kb
    key_pos = last * span + lax.broadcasted_iota(jnp.int32, (span, blk), 0)
    qry_pos = qi * blk + lax.broadcasted_iota(jnp.int32, (span, blk), 1)
    s = [jnp.where(key_pos <= qry_pos, x, NEG) for x in span_scores(last)]
    m = [jnp.max(x, axis=0, keepdims=True) for x in s]
    p = [jnp.exp2(x - mx) for x, mx in zip(s, m)]
    l = [jnp.sum(x, axis=0, keepdims=True) for x in p]
    acc = [jnp.dot(vt_ref[g, last], p[g].astype(BF16), preferred_element_type=F32) for g in groups]

    def past(sp, carry):
        m, l, acc = (list(carry[i * group:(i + 1) * group]) for i in range(3))
        s = span_scores(sp)
        m_new = [jnp.maximum(m[g], jnp.max(s[g], axis=0, keepdims=True)) for g in groups]
        alpha = [jnp.exp2(m[g] - m_new[g]) for g in groups]
        p = [jnp.exp2(s[g] - m_new[g]) for g in groups]
        l = [alpha[g] * l[g] + jnp.sum(p[g], axis=0, keepdims=True) for g in groups]
        pv = [jnp.dot(vt_ref[g, sp], p[g].astype(BF16), preferred_element_type=F32) for g in groups]
        acc = [alpha[g] * acc[g] + pv[g] for g in groups]
        return tuple(m_new + l + acc)

    final = lax.fori_loop(0, last, past, tuple(m + l + acc))
    for g in groups:
        out_ref[g] = (final[2 * group + g] / final[group + g]).T


def _moba_prompt(q, k16, v16, kmean, rel_bias):
    b, s, d = q.shape
    hd = d // MOBA_HEADS
    blk = MOBA_BLOCK
    assert s % blk == 0 and hd == LANES
    nb = s // blk
    assert nb <= LANES
    group = 2 if b % 2 == 0 else 1
    kb = _token_tile(nb, 4)
    n_delta = min(nb, (REL_MAX_DIST + blk - 1) // blk + 2)
    table = _bias_table(rel_bias, n_delta * blk + 1) * LOG2E
    strip_idx = jnp.clip(jnp.arange(n_delta)[:, None] * blk - blk + jnp.arange(2 * blk + 1)[None, :], 0, None)
    bias = _toeplitz_tiles(table[:, strip_idx], blk)
    qspec = pl.BlockSpec((group, blk, hd), lambda bi, hi, qi: (bi, qi, hi))
    kvspec = pl.BlockSpec((group, s, hd), lambda bi, hi, qi: (bi, 0, hi))
    return pl.pallas_call(
        functools.partial(_moba_prompt_kernel, nb=nb, n_delta=n_delta, hd=hd, group=group, kb=kb),
        grid=(b // group, MOBA_HEADS, nb),
        in_specs=[qspec, kvspec, kvspec,
                  pl.BlockSpec((group, nb, hd), lambda bi, hi, qi: (bi, 0, hi)),
                  pl.BlockSpec((1, n_delta, blk, blk), lambda bi, hi, qi: (hi, 0, 0, 0))],
        out_specs=qspec,
        out_shape=jax.ShapeDtypeStruct((b, s, d), F32),
        scratch_shapes=[pltpu.VMEM((group, s, hd + LANES), BF16), pltpu.VMEM((group, nb // kb, hd, kb * blk), BF16),
                        pltpu.VMEM((group, LANES, hd), F32)],
        compiler_params=_cparams("parallel", "arbitrary", "arbitrary"),
        name="moba_prompt",
    )(q, k16, v16, kmean, bias)


def _page_block_mean_kernel(pt_ref, *refs, ppb):
    out_ref = refs[-1]
    pages = refs[:-1]
    for n in range(len(pages) // ppb):
        total = pages[n * ppb][0].astype(F32).mean(axis=0)
        for r in pages[n * ppb + 1:(n + 1) * ppb]:
            total = total + r[0].astype(F32).mean(axis=0)
        out_ref[0, n] = total / ppb


def _page_block_means(cache_k, page_table, nb, ppb):
    n_phys, page, h, hd = cache_k.shape
    db = page_table.shape[0]
    per_step = _token_tile(nb, 4)
    n_in = per_step * ppb

    def page_spec(p):
        return pl.BlockSpec((1, page, h, hd), lambda bi, n, pt: (pt[bi, n * n_in + p], 0, 0, 0))

    return pl.pallas_call(
        functools.partial(_page_block_mean_kernel, ppb=ppb),
        grid_spec=pltpu.PrefetchScalarGridSpec(
            num_scalar_prefetch=1, grid=(db, nb // per_step),
            in_specs=[page_spec(p) for p in range(n_in)],
            out_specs=pl.BlockSpec((1, per_step, h, hd), lambda bi, n, pt: (bi, n, 0, 0))),
        out_shape=jax.ShapeDtypeStruct((db, nb, h, hd), F32),
        compiler_params=_cparams("arbitrary", "arbitrary"),
        name="moba_page_means",
    )(page_table, *([cache_k] * n_in))


def _sample_topk_kernel(q_ref, km_ref, idx_ref, kmp_ref, *, nb, hd):
    lq = q_ref.shape[1]
    lane_f = lax.broadcasted_iota(jnp.int32, (lq, LANES), 1).astype(F32)
    kmp_ref[...] = jnp.zeros_like(kmp_ref)
    for h in range(MOBA_HEADS):
        sl = slice(h * hd, (h + 1) * hd)
        kmp_ref[0:nb, :] = km_ref[0, :, sl]
        g = lax.dot_general(q_ref[0, :, sl], kmp_ref[...], NT, precision=HI, preferred_element_type=F32)
        g = jnp.where(lane_f < nb, g, -jnp.inf)
        out = jnp.zeros((lq, LANES), F32)
        for r in range(MOBA_TOPK):
            m = jnp.max(g, axis=1, keepdims=True)
            idx = jnp.min(jnp.where(g == m, lane_f, 1e9), axis=1, keepdims=True)
            out = jnp.where(lane_f == r, idx, out)
            g = jnp.where(lane_f == idx, -jnp.inf, g)
        idx_ref[0, h] = out.astype(jnp.int32)


def _sample_topk(q, kmean, nb):
    db, lq, d = q.shape
    hd = d // MOBA_HEADS
    out = pl.pallas_call(
        functools.partial(_sample_topk_kernel, nb=nb, hd=hd),
        grid=(db,),
        in_specs=[pl.BlockSpec((1, lq, d), lambda bi: (bi, 0, 0)),
                  pl.BlockSpec((1, nb, d), lambda bi: (bi, 0, 0))],
        out_specs=pl.BlockSpec((1, MOBA_HEADS, lq, LANES), lambda bi: (bi, 0, 0, 0)),
        out_shape=jax.ShapeDtypeStruct((db, MOBA_HEADS, lq, LANES), jnp.int32),
        scratch_shapes=[pltpu.VMEM((LANES, hd), F32)],
        compiler_params=_cparams("parallel"),
        name="moba_sample_topk",
    )(q, kmean)
    return out[..., :MOBA_TOPK]


def _moba_sample_kernel(pt_ref, idx_ref, q_ref, kn_ref, vn_ref, bsel_ref, bown_ref, ck_ref, cv_ref, out_ref,
                        kbuf, vbuf, sem, *, lq, ppb, page, hd):
    bi = pl.program_id(0)
    hi = pl.program_id(1)
    n_heads = pl.num_programs(1)
    step = bi * n_heads + hi
    n_steps = pl.num_programs(0) * n_heads
    n_pick = lq * MOBA_TOPK
    blk = ppb * page

    def copies(b2, h2, slot, i):
        block = idx_ref[(b2 * n_heads + h2) * n_pick + i]
        out = []
        for p in range(ppb):
            phys = pt_ref[b2, block * ppb + p]
            out.append(pltpu.make_async_copy(ck_ref.at[phys, :, h2, :], kbuf.at[slot, i * ppb + p], sem.at[0, slot]))
            out.append(pltpu.make_async_copy(cv_ref.at[phys, :, h2, :], vbuf.at[slot, i * ppb + p], sem.at[1, slot]))
        return out

    def issue(step2, slot):
        b2 = step2 // n_heads
        h2 = step2 % n_heads
        for i in range(n_pick):
            for c in copies(b2, h2, slot, i):
                c.start()

    slot = step % 2

    @pl.when(step == 0)
    def _():
        issue(step, slot)

    @pl.when(step + 1 < n_steps)
    def _():
        issue(step + 1, 1 - slot)

    for i in range(n_pick):
        for c in copies(bi, hi, slot, i):
            c.wait()

    scale = hd ** -0.5
    qb = (q_ref[0] * scale).astype(BF16)
    row = lax.broadcasted_iota(jnp.int32, (lq, 1), 0)
    r_i = lax.broadcasted_iota(jnp.int32, (lq, lq), 0)
    c_i = lax.broadcasted_iota(jnp.int32, (lq, lq), 1)
    s_own = lax.dot_general(qb, kn_ref[0].astype(BF16), NT, preferred_element_type=F32) + bown_ref[0]
    s_own = jnp.where(r_i >= c_i, s_own, NEG)
    m = jnp.max(s_own, axis=1, keepdims=True)
    p_own = jnp.exp(s_own - m)
    l = jnp.sum(p_own, axis=1, keepdims=True)
    acc = jnp.dot(p_own.astype(BF16), vn_ref[0].astype(BF16), preferred_element_type=F32)
    n_rows = MOBA_TOPK * blk
    for qq in range(lq):
        first = qq * MOBA_TOPK * ppb
        kq = kbuf[slot, first:first + MOBA_TOPK * ppb].reshape(n_rows, hd).astype(BF16)
        vq = vbuf[slot, first:first + MOBA_TOPK * ppb].reshape(n_rows, hd).astype(BF16)
        bias = jnp.concatenate(
            [bsel_ref[0, qq, pl.ds(idx_ref[(bi * n_heads + hi) * n_pick + qq * MOBA_TOPK + t], 1), :]
             for t in range(MOBA_TOPK)], axis=1)
        s = lax.dot_general(qb, kq, NT, preferred_element_type=F32) + bias
        mine = row == qq
        m_new = jnp.where(mine, jnp.maximum(m, jnp.max(s, axis=1, keepdims=True)), m)
        alpha = jnp.exp(m - m_new)
        p = jnp.where(mine, jnp.exp(s - m_new), 0.0)
        l = alpha * l + jnp.sum(p, axis=1, keepdims=True)
        acc = alpha * acc + jnp.dot(p.astype(BF16), vq, preferred_element_type=F32)
        m = m_new
    out_ref[0] = acc / l


def _moba_sample(q, k_new, v_new, cache_k, cache_v, page_table, rel_bias):
    db, lq, d = q.shape
    n_phys, page, n_heads, hd = cache_k.shape
    assert n_heads == MOBA_HEADS and hd * n_heads == d
    n_pages = page_table.shape[1]
    past = n_pages * page
    ppb = MOBA_BLOCK // page
    nb = past // MOBA_BLOCK
    assert past % MOBA_BLOCK == 0, "new tokens must start a fresh MoBA block"
    assert MOBA_TOPK <= nb <= LANES and lq <= MOBA_BLOCK
    blk = MOBA_BLOCK

    kmean = _page_block_means(cache_k, page_table, nb, ppb).reshape(db, nb, d)
    idx = _sample_topk(q, kmean, nb)

    table = _bias_table(rel_bias, past + lq)
    bias_sel = jnp.stack([table[:, l + 1:past + l + 1][:, ::-1] for l in range(lq)], axis=1)
    bias_sel = bias_sel.reshape(n_heads, lq, nb, blk)
    bias_own = table[:, jnp.clip(jnp.arange(lq)[:, None] - jnp.arange(lq)[None, :], 0, None)]

    tok = pl.BlockSpec((1, lq, hd), lambda bi, hi, pt, ix: (bi, 0, hi))
    n_slots = lq * MOBA_TOPK * ppb
    return pl.pallas_call(
        functools.partial(_moba_sample_kernel, lq=lq, ppb=ppb, page=page, hd=hd),
        grid_spec=pltpu.PrefetchScalarGridSpec(
            num_scalar_prefetch=2, grid=(db, n_heads),
            in_specs=[tok, tok, tok,
                      pl.BlockSpec((1, lq, nb, blk), lambda bi, hi, pt, ix: (hi, 0, 0, 0)),
                      pl.BlockSpec((1, lq, lq), lambda bi, hi, pt, ix: (hi, 0, 0)),
                      pl.BlockSpec(memory_space=pl.ANY),
                      pl.BlockSpec(memory_space=pl.ANY)],
            out_specs=tok,
            scratch_shapes=[pltpu.VMEM((2, n_slots, page, hd), cache_k.dtype),
                            pltpu.VMEM((2, n_slots, page, hd), cache_v.dtype),
                            pltpu.SemaphoreType.DMA((2, 2))]),
        out_shape=jax.ShapeDtypeStruct((db, lq, d), F32),
        compiler_params=_cparams("arbitrary", "arbitrary"),
        name="moba_sample",
    )(page_table, idx.reshape(-1), q, k_new, v_new, bias_sel, bias_own, cache_k, cache_v)


def _trunk(x, p, s0, attend, w):
    b, l, d = x.shape
    t = b * l
    tm = _token_tile(t, 512)
    hd = d // MOBA_HEADS
    h = x.reshape(t, d)

    q, k, v, r, la = _gla_in(h, w["g_mix"][0], w["w_a_in"][0], w["w_a_gate2"][0], w["b_a_gate"][0], tm)
    seq = lambda a: a.reshape(b, l, a.shape[1])
    o, s_new = _gla_chunked(seq(q), seq(k), seq(v), seq(la), s0)
    h = _gla_out(o.reshape(t, -1), r, h, w["g_a_onorm"][0], w["w_a_out"][0], tm)
    h = _ffn(h, w["g_ffn"][0], w["w_ffn_gu"][0], w["w_ffn_down"][0], tm)
    h, k_sh, k16, v_sh, v16, q_b = _ple(
        h, p[0].reshape(t, -1), w["g_ple"][0], w["w_ple_gate"][0], w["w_ple_proj"][0],
        [(w["g_kv"], w["w_kv"][:, :d], True), (w["g_kv"], w["w_kv"][:, d:], True),
         (w["g_mix"][1], w["w_b_q"][0], False)], tm)

    a = attend(seq(q_b), seq(k_sh), seq(v_sh), seq(k16), seq(v16))
    h, comb = _attnout_router(a.reshape(t, d), h, w["w_b_out"][0], w["g_ffn"][1], w["w_router"][0], tm)
    h = _moe(h, comb, w["g_ffn"][1], w["w_moe_gu"][0], w["w_moe_down"][0], tm)
    _, y = _ple(h, p[1].reshape(t, -1), w["g_ple"][1], w["w_ple_gate"][1], w["w_ple_proj"][1],
                [(w["g_final"], None, False)], tm)
    heads = lambda a: a.reshape(b, l, MOBA_HEADS, hd)
    return y.reshape(b, l, d), s_new[None], heads(k_sh), heads(v_sh)


def kernel(x_prompt, x_sample, state_gla, cache_k, cache_v, page_table, p_prompt, p_sample, g_mix, g_ffn, g_ple,
           w_a_in, w_a_gate2, b_a_gate, g_a_onorm, w_a_out, g_kv, w_kv, w_b_q, w_b_out, rel_bias, w_ffn_gu,
           w_ffn_down, w_router, w_moe_gu, w_moe_down, w_ple_proj, w_ple_gate, g_final):
    assert g_mix.shape[0] == 2 and state_gla.shape[0] == 1, "layer 0 is GLA, layer 1 is MoBA"
    w = dict(g_mix=g_mix, g_ffn=g_ffn, g_ple=g_ple, w_a_in=w_a_in, w_a_gate2=w_a_gate2, b_a_gate=b_a_gate,
             g_a_onorm=g_a_onorm, w_a_out=w_a_out, g_kv=g_kv, w_kv=w_kv, w_b_q=w_b_q, w_b_out=w_b_out,
             w_ffn_gu=w_ffn_gu, w_ffn_down=w_ffn_down, w_router=w_router, w_moe_gu=w_moe_gu,
             w_moe_down=w_moe_down, w_ple_proj=w_ple_proj, w_ple_gate=w_ple_gate, g_final=g_final)
    bp = x_prompt.shape[0]
    zeros = jnp.zeros((bp,) + state_gla.shape[2:], F32)
    y_p, st_p, k_p, v_p = _trunk(x_prompt, p_prompt, zeros,
                                 lambda q, k, v, k16, v16: _moba_prompt(
                                     q, k16, v16, _block_means(k, k.shape[1] // MOBA_BLOCK), rel_bias), w)
    y_s, st_s, k_s, v_s = _trunk(x_sample, p_sample, state_gla[0],
                                 lambda q, k, v, k16, v16: _moba_sample(
                                     q, k, v, cache_k, cache_v, page_table, rel_bias), w)
    return (y_p, y_s, st_p, st_s, k_p, v_p, k_s, v_s)
```

```python
import functools
import math

import jax
import jax.numpy as jnp
from jax import lax
from jax.experimental import pallas as pl
from jax.experimental.pallas import tpu as pltpu

F32 = jnp.float32
BF16 = jnp.bfloat16
HI = lax.Precision.HIGHEST

GLA_HEADS = 4
GLA_GATE_NORM = 16.0
GLA_CHUNK = 64
MOBA_HEADS = 8
MOBA_BLOCK = 256
MOBA_TOPK = 3
REL_BUCKETS = 32
REL_MAX_DIST = 4096
MOE_TOP_K = 2
RMS_EPS = 1e-6

LANES = 128
VMEM_LIMIT = 56 * 1024 * 1024
NEG = -1e30
LOG2E = math.log2(math.e)

NT = (((1,), (1,)), ((), ()))
TN = (((0,), (0,)), ((), ()))


def _cparams(*sem):
    return pltpu.CompilerParams(dimension_semantics=sem, vmem_limit_bytes=VMEM_LIMIT)


def _token_tile(t, cap):
    tm = cap
    while t % tm:
        tm //= 2
    return tm


def _rms(x, g):
    return x * lax.rsqrt(jnp.mean(x * x, axis=-1, keepdims=True) + RMS_EPS) * g


def _sigmoid(x):
    return 1.0 / (1.0 + jnp.exp(-x))


def _silu(x):
    return x * _sigmoid(x)


def _row(v):
    return v.reshape(1, -1).astype(F32)


def _gla_in_kernel(h_ref, g_ref, w_ref, wlr_ref, wg2_ref, bg_ref,
                   q_ref, k_ref, v_ref, r_ref, la_ref, *, qk, vd, qscale):
    xn = _rms(h_ref[...], g_ref[...]).astype(BF16)
    proj = jnp.dot(xn, w_ref[...], preferred_element_type=F32)
    q_ref[...] = proj[:, :qk] * qscale
    k_ref[...] = proj[:, qk:2 * qk]
    v_ref[...] = proj[:, 2 * qk:2 * qk + vd]
    r_ref[...] = proj[:, 2 * qk + vd:]
    glr = jnp.dot(xn, wlr_ref[...], preferred_element_type=F32)
    z = jnp.dot(glr, wg2_ref[...], precision=HI, preferred_element_type=F32) + bg_ref[...]
    log_sig = jnp.minimum(z, 0.0) - jnp.log1p(jnp.exp(-jnp.abs(z)))
    la_ref[...] = log_sig / GLA_GATE_NORM


def _gla_in(h, g, w_in, w_gate2, b_gate, tm):
    t, d = h.shape
    qk = w_gate2.shape[1]
    rank = w_gate2.shape[0]
    vd = (w_in.shape[1] - 2 * qk - rank) // 2
    dk = qk // GLA_HEADS
    n_main = 2 * qk + 2 * vd
    w_main = w_in[:, :n_main].astype(BF16)
    w_lr = w_in[:, n_main:].astype(BF16)
    tok = lambda n: pl.BlockSpec((tm, n), lambda i: (i, 0))
    full = lambda a: pl.BlockSpec(a.shape, lambda i: (0,) * a.ndim)
    args = (h, _row(g), w_main, w_lr, w_gate2.astype(F32), _row(b_gate))
    return pl.pallas_call(
        functools.partial(_gla_in_kernel, qk=qk, vd=vd, qscale=dk ** -0.5),
        grid=(t // tm,),
        in_specs=[tok(d)] + [full(a) for a in args[1:]],
        out_specs=[tok(qk), tok(qk), tok(vd), tok(vd), tok(qk)],
        out_shape=[jax.ShapeDtypeStruct((t, n), F32) for n in (qk, qk, vd, vd, qk)],
        compiler_params=_cparams("parallel"),
        name="gla_in",
    )(*args)


def _gla_kernel(q_ref, k_ref, v_ref, la_ref, s0_ref, o_ref, sout_ref, st_ref, *, chunk, n_chunks, mm_dtype):
    t = pl.program_id(2)

    @pl.when(t == 0)
    def _():
        st_ref[...] = s0_ref[0, 0].T

    rows = lax.broadcasted_iota(jnp.int32, (chunk, chunk), 0)
    cols = lax.broadcasted_iota(jnp.int32, (chunk, chunk), 1)
    tri = rows >= cols
    tri_f = tri.astype(F32)
    for c in range(n_chunks):
        sl = slice(c * chunk, (c + 1) * chunk)
        b = jnp.dot(tri_f, la_ref[0, sl, :], precision=HI, preferred_element_type=F32)
        b_last = b[chunk - 1:chunk, :]
        q = q_ref[0, sl, :]
        k = k_ref[0, sl, :]
        v = v_ref[0, sl, :].astype(mm_dtype)
        q_dec = (q * jnp.exp(b)).astype(mm_dtype)
        k_inv = (k * jnp.exp(-b)).astype(mm_dtype)
        k_end = (k * jnp.exp(b_last - b)).astype(mm_dtype)
        attn = lax.dot_general(q_dec, k_inv, NT, preferred_element_type=F32)
        attn = jnp.where(tri, attn, 0.0).astype(mm_dtype)
        st = st_ref[...]
        o = jnp.dot(attn, v, preferred_element_type=F32)
        o = o + lax.dot_general(q_dec, st.astype(mm_dtype), NT, preferred_element_type=F32)
        o_ref[0, sl, :] = o
        st_ref[...] = st * jnp.exp(b_last) + lax.dot_general(v, k_end, TN, preferred_element_type=F32)

    @pl.when(t == pl.num_programs(2) - 1)
    def _():
        sout_ref[0, 0] = st_ref[...].T


def _gla_chunked(q, k, v, la, s0):
    b, l, qk = q.shape
    dk = qk // GLA_HEADS
    dv = v.shape[2] // GLA_HEADS
    chunk = math.gcd(l, GLA_CHUNK)
    tl = _token_tile(l, 512)
    n_chunks = tl // chunk
    mm_dtype = BF16 if chunk % 16 == 0 else F32
    kspec = pl.BlockSpec((1, tl, dk), lambda bi, hi, ti: (bi, ti, hi))
    vspec = pl.BlockSpec((1, tl, dv), lambda bi, hi, ti: (bi, ti, hi))
    sspec = pl.BlockSpec((1, 1, dk, dv), lambda bi, hi, ti: (bi, hi, 0, 0))
    return pl.pallas_call(
        functools.partial(_gla_kernel, chunk=chunk, n_chunks=n_chunks, mm_dtype=mm_dtype),
        grid=(b, GLA_HEADS, l // tl),
        in_specs=[kspec, kspec, vspec, kspec, sspec],
        out_specs=[vspec, sspec],
        out_shape=[jax.ShapeDtypeStruct(v.shape, F32), jax.ShapeDtypeStruct(s0.shape, F32)],
        scratch_shapes=[pltpu.VMEM((dv, dk), F32)],
        compiler_params=_cparams("parallel", "parallel", "arbitrary"),
        name="gla_chunk",
    )(q, k, v, la, s0.astype(F32))


def _gla_out_kernel(o_ref, r_ref, h_ref, g_ref, w_ref, out_ref, *, dv):
    g = g_ref[...]
    parts = []
    for hd in range(GLA_HEADS):
        sl = slice(hd * dv, (hd + 1) * dv)
        parts.append(_rms(o_ref[:, sl], g) * _silu(r_ref[:, sl]))
    y = jnp.concatenate(parts, axis=1).astype(BF16)
    out_ref[...] = h_ref[...] + jnp.dot(y, w_ref[...], preferred_element_type=F32)


def _gla_out(o, r, h, g_onorm, w_out, tm):
    t, d = h.shape
    vd = o.shape[1]
    tok = lambda n: pl.BlockSpec((tm, n), lambda i: (i, 0))
    full = lambda shape: pl.BlockSpec(shape, lambda i: (0,) * len(shape))
    return pl.pallas_call(
        functools.partial(_gla_out_kernel, dv=vd // GLA_HEADS),
        grid=(t // tm,),
        in_specs=[tok(vd), tok(vd), tok(d), full((1, vd // GLA_HEADS)), full(w_out.shape)],
        out_specs=tok(d),
        out_shape=jax.ShapeDtypeStruct((t, d), F32),
        compiler_params=_cparams("parallel"),
        name="gla_out",
    )(o, r, h, _row(g_onorm), w_out.astype(BF16))


def _ffn_kernel(h_ref, g_ref, wg_ref, wu_ref, wd_ref, out_ref, xn_ref, acc_ref):
    j = pl.program_id(1)

    @pl.when(j == 0)
    def _():
        xn_ref[...] = _rms(h_ref[...], g_ref[...]).astype(BF16)
        acc_ref[...] = jnp.zeros_like(acc_ref)

    xn = xn_ref[...]
    gate = jnp.dot(xn, wg_ref[...], preferred_element_type=F32)
    up = jnp.dot(xn, wu_ref[...], preferred_element_type=F32)
    act = (_silu(gate) * up).astype(BF16)
    acc_ref[...] += jnp.dot(act, wd_ref[...], preferred_element_type=F32)

    @pl.when(j == pl.num_programs(1) - 1)
    def _():
        out_ref[...] = h_ref[...] + acc_ref[...]


def _ff_tile(f):
    for nf in (1, 2, 4, 11, 22):
        if f % nf == 0 and (f // nf) % LANES == 0 and f // nf <= 1536:
            return f // nf
    return f


def _ffn(h, g, w_gu, w_down, tm):
    t, d = h.shape
    f = w_down.shape[0]
    tf = _ff_tile(f)
    nf = f // tf
    w_gu = w_gu.astype(BF16)
    return pl.pallas_call(
        _ffn_kernel,
        grid=(t // tm, nf),
        in_specs=[pl.BlockSpec((tm, d), lambda i, j: (i, 0)),
                  pl.BlockSpec((1, d), lambda i, j: (0, 0)),
                  pl.BlockSpec((d, tf), lambda i, j: (0, j)),
                  pl.BlockSpec((d, tf), lambda i, j: (0, nf + j)),
                  pl.BlockSpec((tf, d), lambda i, j: (j, 0))],
        out_specs=pl.BlockSpec((tm, d), lambda i, j: (i, 0)),
        out_shape=jax.ShapeDtypeStruct((t, d), F32),
        scratch_shapes=[pltpu.VMEM((tm, d), BF16), pltpu.VMEM((tm, d), F32)],
        compiler_params=_cparams("parallel", "arbitrary"),
        name="ffn",
    )(h, _row(g), w_gu, w_gu, w_down.astype(BF16))


def _ple_kernel(*refs, heads):
    h_ref, p_ref, g_ref, wgate_ref, wproj_ref = refs[:5]
    n_in = sum(2 if has_w else 1 for has_w, _ in heads)
    head_refs = refs[5:5 + n_in]
    out_refs = refs[5 + n_in:]
    h = h_ref[...]
    gate = _sigmoid(jnp.dot(_rms(h, g_ref[...]).astype(BF16), wgate_ref[...], preferred_element_type=F32))
    emb = jnp.dot(p_ref[...].astype(BF16), wproj_ref[...], preferred_element_type=F32)
    h = h + gate * emb
    out_refs[0][...] = h
    i_in, i_out = 0, 1
    for has_w, copy16 in heads:
        y = _rms(h, head_refs[i_in][...])
        i_in += 1
        if has_w:
            y = jnp.dot(y.astype(BF16), head_refs[i_in][...], preferred_element_type=F32)
            i_in += 1
        out_refs[i_out][...] = y
        i_out += 1
        if copy16:
            out_refs[i_out][...] = y.astype(BF16)
            i_out += 1


def _ple(h, p, g, w_gate, w_proj, heads, tm):
    t, d = h.shape
    tok = lambda n: pl.BlockSpec((tm, n), lambda i: (i, 0))
    full = lambda a: pl.BlockSpec(a.shape, lambda i: (0,) * a.ndim)
    args = [h, p, _row(g), w_gate.astype(BF16), w_proj.astype(BF16)]
    flags = []
    outs = [(d, F32)]
    for gain, w, copy16 in heads:
        args.append(_row(gain))
        if w is not None:
            args.append(w.astype(BF16))
        n = d if w is None else w.shape[1]
        outs.append((n, F32))
        if copy16:
            outs.append((n, BF16))
        flags.append((w is not None, copy16))
    return pl.pallas_call(
        functools.partial(_ple_kernel, heads=tuple(flags)),
        grid=(t // tm,),
        in_specs=[tok(d), tok(p.shape[1])] + [full(a) for a in args[2:]],
        out_specs=[tok(n) for n, _ in outs],
        out_shape=[jax.ShapeDtypeStruct((t, n), dt) for n, dt in outs],
        compiler_params=_cparams("parallel"),
        name="ple",
    )(*args)


def _router_kernel(a_ref, h_ref, wo_ref, g_ref, wr_ref, out_ref, xn_ref, route_ref, *, n_experts):
    h = h_ref[...] + jnp.dot(a_ref[...].astype(BF16), wo_ref[...], preferred_element_type=F32)
    out_ref[...] = h
    xn = _rms(h, g_ref[...])
    xn_ref[...] = xn.astype(BF16)
    logits = jnp.dot(xn, wr_ref[...], precision=HI, preferred_element_type=F32)
    lane = lax.broadcasted_iota(jnp.int32, logits.shape, 1).astype(F32)
    lg = jnp.where(lane < n_experts, logits, -jnp.inf)
    m1 = jnp.max(lg, axis=1, keepdims=True)
    i1 = jnp.min(jnp.where(lg == m1, lane, 1e9), axis=1, keepdims=True)
    lg2 = jnp.where(lane == i1, -jnp.inf, lg)
    m2 = jnp.max(lg2, axis=1, keepdims=True)
    i2 = jnp.min(jnp.where(lg2 == m2, lane, 1e9), axis=1, keepdims=True)
    e2 = jnp.exp(m2 - m1)
    w1 = 1.0 / (1.0 + e2)
    w2 = e2 / (1.0 + e2)
    route_ref[...] = (jnp.where(lane == 0, i1, 0.0) + jnp.where(lane == 1, i2, 0.0)
                      + jnp.where(lane == 2, w1, 0.0) + jnp.where(lane == 3, w2, 0.0))


def _attnout_router(a, h, w_out, g, w_router, tm):
    t, d = h.shape
    n_experts = w_router.shape[1]
    assert n_experts <= LANES and MOE_TOP_K == 2
    wr = jnp.zeros((d, LANES), F32).at[:, :n_experts].set(w_router.astype(F32))
    tok = lambda n: pl.BlockSpec((tm, n), lambda i: (i, 0))
    full = lambda shape: pl.BlockSpec(shape, lambda i: (0,) * len(shape))
    return pl.pallas_call(
        functools.partial(_router_kernel, n_experts=n_experts),
        grid=(t // tm,),
        in_specs=[tok(d), tok(d), full(w_out.shape), full((1, d)), full(wr.shape)],
        out_specs=[tok(d), tok(d), tok(LANES)],
        out_shape=[jax.ShapeDtypeStruct((t, d), F32), jax.ShapeDtypeStruct((t, d), BF16),
                   jax.ShapeDtypeStruct((t, LANES), F32)],
        compiler_params=_cparams("parallel"),
        name="attnout_router",
    )(a, h, w_out.astype(BF16), _row(g), wr)


def _route_plan(route, n_experts, tr, tc):
    t = route.shape[0]
    assert tc <= tr and t % tc == 0 and (2 * t) % tr == 0
    e = route[:, :2].astype(jnp.int32).reshape(-1)
    w = route[:, 2:4].reshape(-1)
    onehot = (e[:, None] == jnp.arange(n_experts, dtype=jnp.int32)[None, :]).astype(jnp.int32)
    csum = jnp.cumsum(onehot, axis=0)
    rank = jnp.take_along_axis(csum, e[:, None], axis=1)[:, 0] - 1
    gsz = (csum[-1] + tr - 1) // tr * tr
    gend = jnp.cumsum(gsz)
    goff = gend - gsz
    pos = goff[e] + rank
    n_tiles = (2 * t) // tr + n_experts
    src = jnp.full((n_tiles * tr,), -1, jnp.int32).at[pos].set(jnp.arange(2 * t, dtype=jnp.int32) // 2)
    w_sorted = jnp.zeros((n_tiles * tr,), F32).at[pos].set(w)
    tile_start = jnp.arange(n_tiles, dtype=jnp.int32) * tr
    active = tile_start < gend[-1]
    tile_expert = jnp.minimum(jnp.searchsorted(gend, tile_start, side="right"), n_experts - 1).astype(jnp.int32)
    src_tiles = src.reshape(n_tiles, tr)
    win_lo = jnp.where(active, src_tiles[:, 0] // tc, 0)
    win_n = jnp.where(active, src_tiles.max(axis=1) // tc - win_lo + 1, 0)
    per_chunk = jnp.concatenate([jnp.zeros((1, n_experts), jnp.int32), csum[2 * tc - 1::2 * tc]], axis=0)
    lo = goff[None, :] + per_chunk[:-1]
    hi = goff[None, :] + per_chunk[1:]
    first, second = lo // tr, (hi - 1) // tr
    tiles = jnp.concatenate([first, second], axis=1)
    valid = jnp.concatenate([hi > lo, (hi > lo) & (second != first)], axis=1)
    order = jnp.argsort(~valid, axis=1, stable=True)
    return dict(pos=pos.reshape(t, 2), src=src[:, None], w_sorted=w_sorted[:, None],
                tile_expert=tile_expert, win_lo=win_lo.astype(jnp.int32), win_n=win_n.astype(jnp.int32),
                items=jnp.take_along_axis(tiles, order, axis=1).astype(jnp.int32),
                n_items=valid.sum(axis=1).astype(jnp.int32))


def _moe_ffn_kernel(te_ref, wlo_ref, wn_ref, src_ref, ws_ref, xn_hbm, wg_ref, wu_ref, wd_ref, y_ref,
                    xs_ref, acc_ref, buf_ref, sem, *, tc):
    i = pl.program_id(0)
    j = pl.program_id(1)
    n_win = wn_ref[i]
    tr = xs_ref.shape[0]

    def chunk_copy(c, slot):
        start = pl.multiple_of((wlo_ref[i] + c) * tc, tc)
        return pltpu.make_async_copy(xn_hbm.at[pl.ds(start, tc), :], buf_ref.at[slot], sem.at[slot])

    @pl.when(j == 0)
    def _():
        acc_ref[...] = jnp.zeros_like(acc_ref)

        @pl.when(n_win > 0)
        def _():
            chunk_copy(0, 0).start()

        lane = lax.broadcasted_iota(jnp.int32, (tr, tc), 1)

        def body(c, carry):
            slot = c % 2
            chunk_copy(c, slot).wait()

            @pl.when(c + 1 < n_win)
            def _():
                chunk_copy(c + 1, 1 - slot).start()

            pick = jnp.where(src_ref[...] - (wlo_ref[i] + c) * tc == lane, 1.0, 0.0).astype(BF16)
            acc_ref[...] += jnp.dot(pick, buf_ref[slot], preferred_element_type=F32)
            return carry
        lax.fori_loop(0, n_win, body, 0)
        xs_ref[...] = acc_ref[...].astype(BF16)
        acc_ref[...] = jnp.zeros_like(acc_ref)

    @pl.when(n_win > 0)
    def _():
        xs = xs_ref[...]
        gate = jnp.dot(xs, wg_ref[0], preferred_element_type=F32)
        up = jnp.dot(xs, wu_ref[0], preferred_element_type=F32)
        act = (_silu(gate) * up).astype(BF16)
        acc_ref[...] += jnp.dot(act, wd_ref[0], preferred_element_type=F32)

    @pl.when(j == pl.num_programs(1) - 1)
    def _():
        y_ref[...] = (acc_ref[...] * ws_ref[...]).astype(BF16)


def _moe_combine_kernel(items_ref, nitems_ref, h_ref, pos_ref, y_hbm, out_ref, acc_ref, buf_ref, sem, *, max_items):
    c = pl.program_id(0)
    n = nitems_ref[c]
    tc = acc_ref.shape[0]
    tr = buf_ref.shape[1]

    def tile_copy(k, slot):
        start = pl.multiple_of(items_ref[c * max_items + k] * tr, tr)
        return pltpu.make_async_copy(y_hbm.at[pl.ds(start, tr), :], buf_ref.at[slot], sem.at[slot])

    acc_ref[...] = jnp.zeros_like(acc_ref)

    @pl.when(n > 0)
    def _():
        tile_copy(0, 0).start()

    lane = lax.broadcasted_iota(jnp.int32, (tc, tr), 1)

    def body(k, carry):
        slot = k % 2
        tile_copy(k, slot).wait()

        @pl.when(k + 1 < n)
        def _():
            tile_copy(k + 1, 1 - slot).start()

        base = items_ref[c * max_items + k] * tr
        first = pos_ref[:, 0:1] - base
        second = pos_ref[:, 1:2] - base
        pick = jnp.where(first == lane, 1.0, jnp.where(second == lane, 1.0, 0.0)).astype(BF16)
        acc_ref[...] += jnp.dot(pick, buf_ref[slot], preferred_element_type=F32)
        return carry
    lax.fori_loop(0, n, body, 0)
    out_ref[...] = h_ref[...] + acc_ref[...]


def _moe(h, xn, route, w_gu, w_down):
    t, d = h.shape
    n_experts, f, _ = w_down.shape
    tr = tc = _token_tile(t, 512)
    plan = _route_plan(route, n_experts, tr, tc)
    n_tiles = plan["tile_expert"].shape[0]
    tf = _ff_tile(f)
    nf = f // tf
    w_gu = w_gu.astype(BF16)
    rows = lambda n: pl.BlockSpec((tr, n), lambda i, j, te, lo, nw: (i, 0))
    y = pl.pallas_call(
        functools.partial(_moe_ffn_kernel, tc=tc),
        grid_spec=pltpu.PrefetchScalarGridSpec(
            num_scalar_prefetch=3, grid=(n_tiles, nf),
            in_specs=[rows(1), rows(1), pl.BlockSpec(memory_space=pl.ANY),
                      pl.BlockSpec((1, d, tf), lambda i, j, te, lo, nw: (te[i], 0, j)),
                      pl.BlockSpec((1, d, tf), lambda i, j, te, lo, nw: (te[i], 0, nf + j)),
                      pl.BlockSpec((1, tf, d), lambda i, j, te, lo, nw: (te[i], j, 0))],
            out_specs=rows(d),
            scratch_shapes=[pltpu.VMEM((tr, d), BF16), pltpu.VMEM((tr, d), F32), pltpu.VMEM((2, tc, d), BF16),
                            pltpu.SemaphoreType.DMA((2,))]),
        out_shape=jax.ShapeDtypeStruct((n_tiles * tr, d), BF16),
        compiler_params=_cparams("arbitrary", "arbitrary"),
        name="moe_experts",
    )(plan["tile_expert"], plan["win_lo"], plan["win_n"], plan["src"], plan["w_sorted"], xn, w_gu, w_gu,
      w_down.astype(BF16))
    max_items = plan["items"].shape[1]
    return pl.pallas_call(
        functools.partial(_moe_combine_kernel, max_items=max_items),
        grid_spec=pltpu.PrefetchScalarGridSpec(
            num_scalar_prefetch=2, grid=(t // tc,),
            in_specs=[pl.BlockSpec((tc, d), lambda c, it, ni: (c, 0)),
                      pl.BlockSpec((tc, 2), lambda c, it, ni: (c, 0)),
                      pl.BlockSpec(memory_space=pl.ANY)],
            out_specs=pl.BlockSpec((tc, d), lambda c, it, ni: (c, 0)),
            scratch_shapes=[pltpu.VMEM((tc, d), F32), pltpu.VMEM((2, tr, d), BF16), pltpu.SemaphoreType.DMA((2,))]),
        out_shape=jax.ShapeDtypeStruct((t, d), F32),
        compiler_params=_cparams("arbitrary"),
        name="moe_combine",
    )(plan["items"].reshape(-1), plan["n_items"], h, plan["pos"], y)


def _t5_bucket(dist):
    n = jnp.maximum(dist, 0)
    max_exact = REL_BUCKETS // 2
    large = max_exact + (jnp.log(jnp.maximum(n, 1).astype(F32) / max_exact)
                         / math.log(REL_MAX_DIST / max_exact) * (REL_BUCKETS - max_exact)).astype(jnp.int32)
    return jnp.where(n < max_exact, n, jnp.minimum(large, REL_BUCKETS - 1))


def _bias_table(rel_bias, n):
    return rel_bias.astype(F32)[_t5_bucket(jnp.arange(n))].T


def _toeplitz_tiles(strips, blk):
    lead = strips.shape[:-1]
    w = 2 * blk + 1
    flat = jnp.broadcast_to(strips[..., None, :], lead + (blk, w)).reshape(lead + (blk * w,))
    return flat[..., :blk * (w - 1)].reshape(lead + (blk, w - 1))[..., blk:]


def _kmean_kernel(k_ref, out_ref):
    out_ref[0, 0] = jnp.mean(k_ref[0], axis=0, keepdims=True)


def _block_means(k, nb):
    b, s, d = k.shape
    out = pl.pallas_call(
        _kmean_kernel,
        grid=(b, nb),
        in_specs=[pl.BlockSpec((1, MOBA_BLOCK, d), lambda bi, n: (bi, n, 0))],
        out_specs=pl.BlockSpec((1, 1, 1, d), lambda bi, n: (bi, n, 0, 0)),
        out_shape=jax.ShapeDtypeStruct((b, nb, 1, d), F32),
        compiler_params=_cparams("parallel", "parallel"),
        name="moba_kmean",
    )(k)
    return out.reshape(b, nb, d)


def _moba_prompt_kernel(q_ref, k_ref, v_ref, km_ref, bias_ref, out_ref, kaug_ref, vt_ref, kmp_ref,
                        *, nb, n_delta, hd, group, kb):
    qi = pl.program_id(2)
    blk = MOBA_BLOCK
    span = kb * blk
    groups = range(group)

    @pl.when(qi == 0)
    def _():
        lane_i = lax.broadcasted_iota(jnp.int32, (blk, LANES), 1)

        def fill(sp, carry):
            for b in range(kb):
                j = sp * kb + b
                rows = pl.ds(pl.multiple_of(j * blk, blk), blk)
                onehot = jnp.where(lane_i == j, 1.0, 0.0).astype(BF16)
                for g in groups:
                    kaug_ref[g, rows, 0:hd] = k_ref[g, rows, :]
                    kaug_ref[g, rows, hd:hd + LANES] = onehot
                    vt_ref[g, sp, :, b * blk:(b + 1) * blk] = v_ref[g, rows, :].astype(F32).T.astype(BF16)
            return carry
        lax.fori_loop(0, nb // kb, fill, 0)
        kmp_ref[...] = jnp.zeros_like(kmp_ref)
        for g in groups:
            kmp_ref[g, 0:nb, :] = km_ref[g]

    nbp = kmp_ref.shape[1]
    blk_i = lax.broadcasted_iota(jnp.int32, (nbp, blk), 0)
    blk_f = blk_i.astype(F32)
    qscale = hd ** -0.5 * LOG2E

    q_aug = []
    for g in groups:
        q = q_ref[g]
        gate = lax.dot_general(kmp_ref[g], q, NT, precision=HI, preferred_element_type=F32)
        gate = jnp.where(blk_i < qi, gate, -jnp.inf)
        sel = blk_i == qi
        for _ in range(MOBA_TOPK):
            best = jnp.max(gate, axis=0, keepdims=True)
            idx = jnp.min(jnp.where(gate == best, blk_f, 1e9), axis=0, keepdims=True)
            hit = blk_f == idx
            sel = sel | (hit & (blk_i < qi))
            gate = jnp.where(hit, -jnp.inf, gate)
        sel_bias = jnp.where(sel, 0.0, NEG)
        if nbp < LANES:
            sel_bias = jnp.concatenate([sel_bias, jnp.zeros((LANES - nbp, blk), F32)], axis=0)
        sel_bias = sel_bias.T
        q_aug.append(jnp.concatenate([(q * qscale).astype(BF16), sel_bias.astype(BF16)], axis=1))

    def span_scores(sp):
        rows = pl.ds(pl.multiple_of(sp * span, span), span)
        bias = jnp.concatenate(
            [bias_ref[0, jnp.clip(qi - (sp * kb + b), 0, n_delta - 1)] for b in range(kb)], axis=0)
        return [lax.dot_general(kaug_ref[g, rows, :], q_aug[g], NT, preferred_element_type=F32) + bias
                for g in groups]

    last = qi // kb
    key_pos = last * span + lax.broadcasted_iota(jnp.int32, (span, blk), 0)
    qry_pos = qi * blk + lax.broadcasted_iota(jnp.int32, (span, blk), 1)
    s = [jnp.where(key_pos <= qry_pos, x, NEG) for x in span_scores(last)]
    m = [jnp.max(x, axis=0, keepdims=True) for x in s]
    p = [jnp.exp2(x - mx) for x, mx in zip(s, m)]
    l = [jnp.sum(x, axis=0, keepdims=True) for x in p]
    acc = [jnp.dot(vt_ref[g, last], p[g].astype(BF16), preferred_element_type=F32) for g in groups]

    def past(sp, carry):
        m, l, acc = (list(carry[i * group:(i + 1) * group]) for i in range(3))
        s = span_scores(sp)
        m_new = [jnp.maximum(m[g], jnp.max(s[g], axis=0, keepdims=True)) for g in groups]
        alpha = [jnp.exp2(m[g] - m_new[g]) for g in groups]
        p = [jnp.exp2(s[g] - m_new[g]) for g in groups]
        l = [alpha[g] * l[g] + jnp.sum(p[g], axis=0, keepdims=True) for g in groups]
        pv = [jnp.dot(vt_ref[g, sp], p[g].astype(BF16), preferred_element_type=F32) for g in groups]
        acc = [alpha[g] * acc[g] + pv[g] for g in groups]
        return tuple(m_new + l + acc)

    final = lax.fori_loop(0, last, past, tuple(m + l + acc))
    for g in groups:
        out_ref[g] = (final[2 * group + g] / final[group + g]).T


def _moba_prompt(q, k16, v16, kmean, rel_bias):
    b, s, d = q.shape
    hd = d // MOBA_HEADS
    blk = MOBA_BLOCK
    assert s % blk == 0 and hd == LANES
    nb = s // blk
    assert nb <= LANES
    group = 2 if b % 2 == 0 else 1
    kb = _token_tile(nb, 4)
    n_delta = min(nb, (REL_MAX_DIST + blk - 1) // blk + 2)
    table = _bias_table(rel_bias, n_delta * blk + 1) * LOG2E
    strip_idx = jnp.clip(jnp.arange(n_delta)[:, None] * blk - blk + jnp.arange(2 * blk + 1)[None, :], 0, None)
    bias = _toeplitz_tiles(table[:, strip_idx], blk)
    qspec = pl.BlockSpec((group, blk, hd), lambda bi, hi, qi: (bi, qi, hi))
    kvspec = pl.BlockSpec((group, s, hd), lambda bi, hi, qi: (bi, 0, hi))
    return pl.pallas_call(
        functools.partial(_moba_prompt_kernel, nb=nb, n_delta=n_delta, hd=hd, group=group, kb=kb),
        grid=(b // group, MOBA_HEADS, nb),
        in_specs=[qspec, kvspec, kvspec,
                  pl.BlockSpec((group, nb, hd), lambda bi, hi, qi: (bi, 0, hi)),
                  pl.BlockSpec((1, n_delta, blk, blk), lambda bi, hi, qi: (hi, 0, 0, 0))],
        out_specs=qspec,
        out_shape=jax.ShapeDtypeStruct((b, s, d), F32),
        scratch_shapes=[pltpu.VMEM((group, s, hd + LANES), BF16), pltpu.VMEM((group, nb // kb, hd, kb * blk), BF16),
                        pltpu.VMEM((group, -(-nb // 8) * 8, hd), F32)],
        compiler_params=_cparams("parallel", "arbitrary", "arbitrary"),
        name="moba_prompt",
    )(q, k16, v16, kmean, bias)


def _page_block_mean_kernel(pt_ref, *refs, ppb):
    out_ref = refs[-1]
    pages = refs[:-1]
    for n in range(len(pages) // ppb):
        total = pages[n * ppb][0].astype(F32).mean(axis=0)
        for r in pages[n * ppb + 1:(n + 1) * ppb]:
            total = total + r[0].astype(F32).mean(axis=0)
        out_ref[0, n] = total / ppb


def _page_block_means(cache_k, page_table, nb, ppb):
    n_phys, page, h, hd = cache_k.shape
    db = page_table.shape[0]
    per_step = _token_tile(nb, 4)
    n_in = per_step * ppb

    def page_spec(p):
        return pl.BlockSpec((1, page, h, hd), lambda bi, n, pt: (pt[bi, n * n_in + p], 0, 0, 0))

    return pl.pallas_call(
        functools.partial(_page_block_mean_kernel, ppb=ppb),
        grid_spec=pltpu.PrefetchScalarGridSpec(
            num_scalar_prefetch=1, grid=(db, nb // per_step),
            in_specs=[page_spec(p) for p in range(n_in)],
            out_specs=pl.BlockSpec((1, per_step, h, hd), lambda bi, n, pt: (bi, n, 0, 0))),
        out_shape=jax.ShapeDtypeStruct((db, nb, h, hd), F32),
        compiler_params=_cparams("arbitrary", "arbitrary"),
        name="moba_page_means",
    )(page_table, *([cache_k] * n_in))


def _sample_topk_kernel(q_ref, km_ref, idx_ref, kmp_ref, *, nb, hd):
    lq = q_ref.shape[1]
    lane_f = lax.broadcasted_iota(jnp.int32, (lq, LANES), 1).astype(F32)
    kmp_ref[...] = jnp.zeros_like(kmp_ref)
    for h in range(MOBA_HEADS):
        sl = slice(h * hd, (h + 1) * hd)
        kmp_ref[0:nb, :] = km_ref[0, :, sl]
        g = lax.dot_general(q_ref[0, :, sl], kmp_ref[...], NT, precision=HI, preferred_element_type=F32)
        g = jnp.where(lane_f < nb, g, -jnp.inf)
        out = jnp.zeros((lq, LANES), F32)
        for r in range(MOBA_TOPK):
            m = jnp.max(g, axis=1, keepdims=True)
            idx = jnp.min(jnp.where(g == m, lane_f, 1e9), axis=1, keepdims=True)
            out = jnp.where(lane_f == r, idx, out)
            g = jnp.where(lane_f == idx, -jnp.inf, g)
        idx_ref[0, h] = out.astype(jnp.int32)


def _sample_topk(q, kmean, nb):
    db, lq, d = q.shape
    hd = d // MOBA_HEADS
    out = pl.pallas_call(
        functools.partial(_sample_topk_kernel, nb=nb, hd=hd),
        grid=(db,),
        in_specs=[pl.BlockSpec((1, lq, d), lambda bi: (bi, 0, 0)),
                  pl.BlockSpec((1, nb, d), lambda bi: (bi, 0, 0))],
        out_specs=pl.BlockSpec((1, MOBA_HEADS, lq, LANES), lambda bi: (bi, 0, 0, 0)),
        out_shape=jax.ShapeDtypeStruct((db, MOBA_HEADS, lq, LANES), jnp.int32),
        scratch_shapes=[pltpu.VMEM((LANES, hd), F32)],
        compiler_params=_cparams("parallel"),
        name="moba_sample_topk",
    )(q, kmean)
    return out[..., :MOBA_TOPK]


def _moba_sample_kernel(pt_ref, idx_ref, q_ref, kn_ref, vn_ref, bsel_ref, bown_ref, ck_ref, cv_ref, out_ref,
                        kbuf, vbuf, sem, *, lq, ppb, page, hd):
    bi = pl.program_id(0)
    hi = pl.program_id(1)
    n_heads = pl.num_programs(1)
    step = bi * n_heads + hi
    n_steps = pl.num_programs(0) * n_heads
    n_pick = lq * MOBA_TOPK
    blk = ppb * page

    def copies(b2, h2, slot, i):
        block = idx_ref[(b2 * n_heads + h2) * n_pick + i]
        out = []
        for p in range(ppb):
            phys = pt_ref[b2, block * ppb + p]
            out.append(pltpu.make_async_copy(ck_ref.at[phys, :, h2, :], kbuf.at[slot, i * ppb + p], sem.at[0, slot]))
            out.append(pltpu.make_async_copy(cv_ref.at[phys, :, h2, :], vbuf.at[slot, i * ppb + p], sem.at[1, slot]))
        return out

    def issue(step2, slot):
        b2 = step2 // n_heads
        h2 = step2 % n_heads
        for i in range(n_pick):
            for c in copies(b2, h2, slot, i):
                c.start()

    slot = step % 2

    @pl.when(step == 0)
    def _():
        issue(step, slot)

    @pl.when(step + 1 < n_steps)
    def _():
        issue(step + 1, 1 - slot)

    for i in range(n_pick):
        for c in copies(bi, hi, slot, i):
            c.wait()

    scale = hd ** -0.5
    qb = (q_ref[0] * scale).astype(BF16)
    row = lax.broadcasted_iota(jnp.int32, (lq, 1), 0)
    r_i = lax.broadcasted_iota(jnp.int32, (lq, lq), 0)
    c_i = lax.broadcasted_iota(jnp.int32, (lq, lq), 1)
    s_own = lax.dot_general(qb, kn_ref[0].astype(BF16), NT, preferred_element_type=F32) + bown_ref[0]
    s_own = jnp.where(r_i >= c_i, s_own, NEG)
    m = jnp.max(s_own, axis=1, keepdims=True)
    p_own = jnp.exp(s_own - m)
    l = jnp.sum(p_own, axis=1, keepdims=True)
    acc = jnp.dot(p_own.astype(BF16), vn_ref[0].astype(BF16), preferred_element_type=F32)
    n_rows = MOBA_TOPK * blk
    for qq in range(lq):
        first = qq * MOBA_TOPK * ppb
        kq = kbuf[slot, first:first + MOBA_TOPK * ppb].reshape(n_rows, hd).astype(BF16)
        vq = vbuf[slot, first:first + MOBA_TOPK * ppb].reshape(n_rows, hd).astype(BF16)
        bias = jnp.concatenate(
            [bsel_ref[0, qq, pl.ds(idx_ref[(bi * n_heads + hi) * n_pick + qq * MOBA_TOPK + t], 1), :]
             for t in range(MOBA_TOPK)], axis=1)
        s = lax.dot_general(qb, kq, NT, preferred_element_type=F32) + bias
        mine = row == qq
        m_new = jnp.where(mine, jnp.maximum(m, jnp.max(s, axis=1, keepdims=True)), m)
        alpha = jnp.exp(m - m_new)
        p = jnp.where(mine, jnp.exp(s - m_new), 0.0)
        l = alpha * l + jnp.sum(p, axis=1, keepdims=True)
        acc = alpha * acc + jnp.dot(p.astype(BF16), vq, preferred_element_type=F32)
        m = m_new
    out_ref[0] = acc / l


def _moba_sample(q, k_new, v_new, cache_k, cache_v, page_table, rel_bias):
    db, lq, d = q.shape
    n_phys, page, n_heads, hd = cache_k.shape
    assert n_heads == MOBA_HEADS and hd * n_heads == d
    n_pages = page_table.shape[1]
    past = n_pages * page
    ppb = MOBA_BLOCK // page
    nb = past // MOBA_BLOCK
    assert past % MOBA_BLOCK == 0, "new tokens must start a fresh MoBA block"
    assert MOBA_TOPK <= nb <= LANES and lq <= MOBA_BLOCK
    blk = MOBA_BLOCK

    kmean = _page_block_means(cache_k, page_table, nb, ppb).reshape(db, nb, d)
    idx = _sample_topk(q, kmean, nb)

    n_dist = past + lq
    table_rev = rel_bias.astype(F32)[_t5_bucket(n_dist - 1 - jnp.arange(n_dist))].T
    bias_sel = jnp.stack([table_rev[:, lq - 1 - l:lq - 1 - l + past] for l in range(lq)], axis=1)
    bias_sel = bias_sel.reshape(n_heads, lq, nb, blk)
    bias_own = _bias_table(rel_bias, lq)[:, jnp.clip(jnp.arange(lq)[:, None] - jnp.arange(lq)[None, :], 0, None)]

    tok = pl.BlockSpec((1, lq, hd), lambda bi, hi, pt, ix: (bi, 0, hi))
    n_slots = lq * MOBA_TOPK * ppb
    return pl.pallas_call(
        functools.partial(_moba_sample_kernel, lq=lq, ppb=ppb, page=page, hd=hd),
        grid_spec=pltpu.PrefetchScalarGridSpec(
            num_scalar_prefetch=2, grid=(db, n_heads),
            in_specs=[tok, tok, tok,
                      pl.BlockSpec((1, lq, nb, blk), lambda bi, hi, pt, ix: (hi, 0, 0, 0)),
                      pl.BlockSpec((1, lq, lq), lambda bi, hi, pt, ix: (hi, 0, 0)),
                      pl.BlockSpec(memory_space=pl.ANY),
                      pl.BlockSpec(memory_space=pl.ANY)],
            out_specs=tok,
            scratch_shapes=[pltpu.VMEM((2, n_slots, page, hd), cache_k.dtype),
                            pltpu.VMEM((2, n_slots, page, hd), cache_v.dtype),
                            pltpu.SemaphoreType.DMA((2, 2))]),
        out_shape=jax.ShapeDtypeStruct((db, lq, d), F32),
        compiler_params=_cparams("arbitrary", "arbitrary"),
        name="moba_sample",
    )(page_table, idx.reshape(-1), q, k_new, v_new, bias_sel, bias_own, cache_k, cache_v)


def _trunk(x, p, s0, attend, w):
    b, l, d = x.shape
    t = b * l
    tm = _token_tile(t, 512)
    hd = d // MOBA_HEADS
    h = x.reshape(t, d)

    q, k, v, r, la = _gla_in(h, w["g_mix"][0], w["w_a_in"][0], w["w_a_gate2"][0], w["b_a_gate"][0], tm)
    seq = lambda a: a.reshape(b, l, a.shape[1])
    o, s_new = _gla_chunked(seq(q), seq(k), seq(v), seq(la), s0)
    h = _gla_out(o.reshape(t, -1), r, h, w["g_a_onorm"][0], w["w_a_out"][0], tm)
    h = _ffn(h, w["g_ffn"][0], w["w_ffn_gu"][0], w["w_ffn_down"][0], tm)
    h, k_sh, k16, v_sh, v16, q_b = _ple(
        h, p[0].reshape(t, -1), w["g_ple"][0], w["w_ple_gate"][0], w["w_ple_proj"][0],
        [(w["g_kv"], w["w_kv"][:, :d], True), (w["g_kv"], w["w_kv"][:, d:], True),
         (w["g_mix"][1], w["w_b_q"][0], False)], tm)

    a = attend(seq(q_b), seq(k_sh), seq(v_sh), seq(k16), seq(v16))
    h, xn, route = _attnout_router(a.reshape(t, d), h, w["w_b_out"][0], w["g_ffn"][1], w["w_router"][0], tm)
    h = _moe(h, xn, route, w["w_moe_gu"][0], w["w_moe_down"][0])
    _, y = _ple(h, p[1].reshape(t, -1), w["g_ple"][1], w["w_ple_gate"][1], w["w_ple_proj"][1],
                [(w["g_final"], None, False)], tm)
    heads = lambda a: a.reshape(b, l, MOBA_HEADS, hd)
    return y.reshape(b, l, d), s_new[None], heads(k_sh), heads(v_sh)


def kernel(x_prompt, x_sample, state_gla, cache_k, cache_v, page_table, p_prompt, p_sample, g_mix, g_ffn, g_ple,
           w_a_in, w_a_gate2, b_a_gate, g_a_onorm, w_a_out, g_kv, w_kv, w_b_q, w_b_out, rel_bias, w_ffn_gu,
           w_ffn_down, w_router, w_moe_gu, w_moe_down, w_ple_proj, w_ple_gate, g_final):
    assert g_mix.shape[0] == 2 and state_gla.shape[0] == 1, "layer 0 is GLA, layer 1 is MoBA"
    w = dict(g_mix=g_mix, g_ffn=g_ffn, g_ple=g_ple, w_a_in=w_a_in, w_a_gate2=w_a_gate2, b_a_gate=b_a_gate,
             g_a_onorm=g_a_onorm, w_a_out=w_a_out, g_kv=g_kv, w_kv=w_kv, w_b_q=w_b_q, w_b_out=w_b_out,
             w_ffn_gu=w_ffn_gu, w_ffn_down=w_ffn_down, w_router=w_router, w_moe_gu=w_moe_gu,
             w_moe_down=w_moe_down, w_ple_proj=w_ple_proj, w_ple_gate=w_ple_gate, g_final=g_final)
    bp = x_prompt.shape[0]
    zeros = jnp.zeros((bp,) + state_gla.shape[2:], F32)
    y_p, st_p, k_p, v_p = _trunk(x_prompt, p_prompt, zeros,
                                 lambda q, k, v, k16, v16: _moba_prompt(
                                     q, k16, v16, _block_means(k, k.shape[1] // MOBA_BLOCK), rel_bias), w)
    y_s, st_s, k_s, v_s = _trunk(x_sample, p_sample, state_gla[0],
                                 lambda q, k, v, k16, v16: _moba_sample(
                                     q, k, v, cache_k, cache_v, page_table, rel_bias), w)
    return (y_p, y_s, st_p, st_s, k_p, v_p, k_s, v_s)
```
